```python
import jax, jax.numpy as jnp
from jax import lax
import numpy as np

D_MODEL = 2048
BATCH = 16
SEQ = 2048
DEPTH = 4
DEC_BATCH = 8
DEC_SEQ = 2048
PAST_LEN = 128

GRID_W = 64
HEAD_DIM = 128
A_Q = D_MODEL // 2
A_HEADS = A_Q // HEAD_DIM
A_KV_HEADS = A_HEADS // 4
A_GROUP = A_HEADS // A_KV_HEADS
A_KV = A_KV_HEADS * HEAD_DIM
ROPE_THETA = 10000.0
Q_BLOCK = 128
B_W = D_MODEL // 4
B_HEADS = B_W // HEAD_DIM
NA_ROWS = 8
NA_COLS = 16
C_V = D_MODEL // 4
C_HEADS = 4
C_DV = C_V // C_HEADS
C_DK = C_DV // 2
C_K = C_HEADS * C_DK
C_RANK = 16
C_TAU = 16.0
C_CHUNK = 64
D_FF = 5632
EPS = 1e-6
N_BRANCH = 3

SPLITS = (A_Q, A_KV, A_KV,
          B_W, B_W, B_W,
          C_K, C_K, C_V, C_V,
          C_RANK, C_RANK,
          D_MODEL, D_MODEL, D_MODEL)
N_IN = sum(SPLITS)
SPLIT_IDX = tuple(int(i) for i in np.cumsum(SPLITS)[:-1])

kernel_name = "hybrid_gqa_natten_gla_macaron_encoder"


def rms_norm(x, gain):
    x32 = x.astype(jnp.float32)
    y = x32 * lax.rsqrt(jnp.mean(x32 * x32, axis=-1, keepdims=True) + EPS)
    return (y * gain.astype(jnp.float32)).astype(x.dtype)


def swiglu(x, w_in, w_out):
    g, u = jnp.split(x @ w_in, 2, axis=-1)
    return (jax.nn.silu(g) * u) @ w_out


def axial_rope(n_tok):
    t = jnp.arange(n_tok)
    pos_r = (t // GRID_W).astype(jnp.float32)
    pos_c = (t % GRID_W).astype(jnp.float32)
    half = HEAD_DIM // 2
    inv = ROPE_THETA ** (-jnp.arange(0, half, 2, dtype=jnp.float32) / half)
    ang = jnp.concatenate([pos_r[:, None] * inv, pos_c[:, None] * inv], axis=-1)
    return jnp.cos(ang), jnp.sin(ang)


def apply_rope(x, cos, sin):
    xp = x.astype(jnp.float32).reshape(*x.shape[:-1], HEAD_DIM // 2, 2)
    x1, x2 = xp[..., 0], xp[..., 1]
    c = cos[None, :, None, :]
    s = sin[None, :, None, :]
    out = jnp.stack([x1 * c - x2 * s, x1 * s + x2 * c], axis=-1)
    return out.reshape(x.shape).astype(x.dtype)


def mixer_a(q, k, v, qk_gain, cos, sin):
    bsz, n_tok, _ = q.shape
    q = q.reshape(bsz, n_tok, A_HEADS, HEAD_DIM)
    k = k.reshape(bsz, n_tok, A_KV_HEADS, HEAD_DIM)
    v = v.reshape(bsz, n_tok, A_KV_HEADS, HEAD_DIM)
    q = apply_rope(rms_norm(q, qk_gain[0]), cos, sin)
    k = apply_rope(rms_norm(k, qk_gain[1]), cos, sin)
    n_blk = n_tok // Q_BLOCK
    qb_all = q.reshape(bsz, n_blk, Q_BLOCK, A_KV_HEADS, A_GROUP, HEAD_DIM).transpose(1, 0, 2, 3, 4, 5)
    scale = HEAD_DIM ** -0.5

    def block(qb):
        s = jnp.einsum('bqkgd,bskd->bkgqs', qb, k, preferred_element_type=jnp.float32) * scale
        p = jax.nn.softmax(s, axis=-1).astype(v.dtype)
        return jnp.einsum('bkgqs,bskd->bqkgd', p, v)

    o = lax.map(block, qb_all)
    return o.transpose(1, 0, 2, 3, 4, 5).reshape(bsz, n_tok, A_Q)


def mixer_b(q, k, v, rpb):
    bsz, n_tok, _ = q.shape
    rows = n_tok // GRID_W
    wr = min(NA_ROWS, rows)
    q = q.reshape(bsz, rows, GRID_W, B_HEADS, HEAD_DIM) * (HEAD_DIM ** -0.5)
    k = k.reshape(bsz, rows, GRID_W, B_HEADS, HEAD_DIM)
    v = v.reshape(bsz, rows, GRID_W, B_HEADS, HEAD_DIM)
    cols = jnp.arange(GRID_W)
    col_start = jnp.clip(cols - NA_COLS // 2, 0, GRID_W - NA_COLS)
    col_idx = col_start[:, None] + jnp.arange(NA_COLS)[None, :]
    dc = col_idx - cols[:, None] + (NA_COLS - 1)

    def row_block(r):
        rs = jnp.clip(r - wr // 2, 0, rows - wr)
        kb = lax.dynamic_slice_in_dim(k, rs, wr, axis=1)
        vb = lax.dynamic_slice_in_dim(v, rs, wr, axis=1)
        kw = kb[:, :, col_idx]
        vw = vb[:, :, col_idx]
        dr = rs + jnp.arange(wr) - r + (NA_ROWS - 1)
        bias = rpb[:, dr[:, None, None], dc[None, :, :]]
        qr = lax.dynamic_index_in_dim(q, r, axis=1, keepdims=False)
        s = jnp.einsum('bqhd,bwqjhd->bhqwj', qr, kw, preferred_element_type=jnp.float32)
        s = s + bias.transpose(0, 2, 1, 3)[None].astype(jnp.float32)
        p = jax.nn.softmax(s.reshape(bsz, B_HEADS, GRID_W, wr * NA_COLS), axis=-1)
        p = p.reshape(bsz, B_HEADS, GRID_W, wr, NA_COLS).astype(v.dtype)
        return jnp.einsum('bhqwj,bwqjhd->bqhd', p, vw)

    o = lax.map(row_block, jnp.arange(rows))
    return o.transpose(1, 0, 2, 3, 4).reshape(bsz, n_tok, B_W)


def gla_scan(q, k, v, g):
    bsz, n_tok, n_h, dk = q.shape
    dv = v.shape[-1]
    n_ch = n_tok // C_CHUNK

    def to_chunks(t):
        return t.reshape(bsz, n_ch, C_CHUNK, n_h, t.shape[-1]).transpose(1, 0, 3, 2, 4)

    mask = jnp.tril(jnp.ones((C_CHUNK, C_CHUNK), dtype=bool))

    def step(state, inp):
        qi, ki, vi, gi = inp
        b = jnp.cumsum(gi, axis=-2)
        diff = b[:, :, :, None, :] - b[:, :, None, :, :]
        decay = jnp.exp(jnp.where(mask[:, :, None], diff, -jnp.inf))
        attn = jnp.einsum('bhid,bhjd,bhijd->bhij', qi, ki, decay)
        o = jnp.einsum('bhij,bhjv->bhiv', attn, vi) + jnp.einsum('bhid,bhdv->bhiv', qi * jnp.exp(b), state)
        b_last = b[:, :, -1:, :]
        state = jnp.exp(b_last[:, :, 0, :, None]) * state + jnp.einsum('bhjd,bhjv->bhdv', ki * jnp.exp(b_last - b), vi)
        return state, o

    state0 = jnp.zeros((bsz, n_h, dk, dv), jnp.float32)
    _, o = lax.scan(step, state0, (to_chunks(q), to_chunks(k), to_chunks(v), to_chunks(g)))
    return o.transpose(1, 0, 3, 2, 4).reshape(bsz, n_tok, n_h, dv)


def mixer_c(xq, xk, xv, xog, lr_f, lr_b, w_decay, b_decay, onorm):
    bsz, n_tok, _ = xq.shape
    f32 = jnp.float32
    q = xq.astype(f32).reshape(bsz, n_tok, C_HEADS, C_DK) * (C_DK ** -0.5)
    k = xk.astype(f32).reshape(bsz, n_tok, C_HEADS, C_DK)
    v = xv.astype(f32).reshape(bsz, n_tok, C_HEADS, C_DV)

    def log_decay(lr, w2, b2):
        z = lr.astype(f32) @ w2.astype(f32) + b2.astype(f32)
        return (jax.nn.log_sigmoid(z) / C_TAU).reshape(bsz, n_tok, C_HEADS, C_DK)

    g_f = log_decay(lr_f, w_decay[0], b_decay[0])
    g_b = log_decay(lr_b, w_decay[1], b_decay[1])
    flip = lambda t: jnp.flip(t, axis=1)
    o_f = gla_scan(q, k, v, g_f)
    o_b = flip(gla_scan(flip(q), flip(k), flip(v), flip(g_b)))
    o = rms_norm(o_f + o_b, onorm).reshape(bsz, n_tok, C_V)
    o = o * jax.nn.silu(xog.astype(f32))
    return o.astype(xq.dtype)


def token_mixing(u, w_in, gate_bias, qk_norm_a, rpb_b, w_decay_c, b_decay_c, onorm_c,
                 w_br_a, w_br_b, w_br_c, w_out, cos, sin):
    proj = u @ w_in
    (aq, ak, av, bq, bk, bv, cq, ck, cv, cog, clf, clb, ga, gb, gc) = jnp.split(proj, SPLIT_IDX, axis=-1)
    ya = mixer_a(aq, ak, av, qk_norm_a, cos, sin) @ w_br_a
    yb = mixer_b(bq, bk, bv, rpb_b) @ w_br_b
    yc = mixer_c(cq, ck, cv, cog, clf, clb, w_decay_c, b_decay_c, onorm_c) @ w_br_c
    merged = (jax.nn.sigmoid(ga + gate_bias[0]) * ya
              + jax.nn.sigmoid(gb + gate_bias[1]) * yb
              + jax.nn.sigmoid(gc + gate_bias[2]) * yc)
    return merged @ w_out


def trunk(x, norm_gains, w_in, gate_bias, qk_norm_a, rpb_b, w_decay_c, b_decay_c, onorm_c,
          w_br_a, w_br_b, w_br_c, w_out, w_ffn1_in, w_ffn1_out, w_ffn2_in, w_ffn2_out):
    cos, sin = axial_rope(x.shape[1])
    for l in range(DEPTH):
        ng = norm_gains[l]
        x = x + 0.5 * rms_norm(swiglu(rms_norm(x, ng[0]), w_ffn1_in[l], w_ffn1_out[l]), ng[1])
        mix = token_mixing(rms_norm(x, ng[2]), w_in[l], gate_bias[l], qk_norm_a[l], rpb_b[l],
                           w_decay_c[l], b_decay_c[l], onorm_c[l], w_br_a[l], w_br_b[l], w_br_c[l],
                           w_out[l], cos, sin)
        x = x + rms_norm(mix, ng[3])
        x = x + 0.5 * rms_norm(swiglu(rms_norm(x, ng[4]), w_ffn2_in[l], w_ffn2_out[l]), ng[5])
    return x


def setup_inputs(seed: int = 0) -> dict:
    key = jax.random.key(seed)
    ks = jax.random.split(key, 20)
    f32 = jnp.float32

    def dense(k, shape, fan_in):
        return jax.random.normal(k, shape, f32) * (fan_in ** -0.5)

    return {
        "x_prompt": jax.random.normal(ks[0], (BATCH, SEQ, D_MODEL), f32),
        "x_sample": jax.random.normal(ks[1], (DEC_BATCH, DEC_SEQ, D_MODEL), f32),
        "norm_gains": 1.0 + 0.02 * jax.random.normal(ks[2], (DEPTH, 6, D_MODEL), f32),
        "w_in": dense(ks[3], (DEPTH, D_MODEL, N_IN), D_MODEL),
        "gate_bias": 0.02 * jax.random.normal(ks[4], (DEPTH, N_BRANCH, D_MODEL), f32),
        "qk_norm_a": 1.0 + 0.02 * jax.random.normal(ks[5], (DEPTH, 2, HEAD_DIM), f32),
        "rpb_b": 0.1 * jax.random.normal(ks[6], (DEPTH, B_HEADS, 2 * NA_ROWS - 1, 2 * NA_COLS - 1), f32),
        "w_decay_c": dense(ks[7], (DEPTH, 2, C_RANK, C_K), C_RANK),
        "b_decay_c": 0.1 * jax.random.normal(ks[8], (DEPTH, 2, C_K), f32),
        "onorm_c": 1.0 + 0.02 * jax.random.normal(ks[9], (DEPTH, C_DV), f32),
        "w_br_a": dense(ks[10], (DEPTH, A_Q, D_MODEL), A_Q),
        "w_br_b": dense(ks[11], (DEPTH, B_W, D_MODEL), B_W),
        "w_br_c": dense(ks[12], (DEPTH, C_V, D_MODEL), C_V),
        "w_out": dense(ks[13], (DEPTH, D_MODEL, D_MODEL), D_MODEL),
        "w_ffn1_in": dense(ks[14], (DEPTH, D_MODEL, 2 * D_FF), D_MODEL),
        "w_ffn1_out": dense(ks[15], (DEPTH, D_FF, D_MODEL), D_FF),
        "w_ffn2_in": dense(ks[16], (DEPTH, D_MODEL, 2 * D_FF), D_MODEL),
        "w_ffn2_out": dense(ks[17], (DEPTH, D_FF, D_MODEL), D_FF),
    }


def reference(x_prompt, x_sample, norm_gains, w_in, gate_bias, qk_norm_a, rpb_b, w_decay_c, b_decay_c,
              onorm_c, w_br_a, w_br_b, w_br_c, w_out, w_ffn1_in, w_ffn1_out, w_ffn2_in, w_ffn2_out):
    y_prompt = trunk(x_prompt, norm_gains, w_in, gate_bias, qk_norm_a, rpb_b, w_decay_c, b_decay_c, onorm_c,
                     w_br_a, w_br_b, w_br_c, w_out, w_ffn1_in, w_ffn1_out, w_ffn2_in, w_ffn2_out)
    y_sample = trunk(x_sample, norm_gains, w_in, gate_bias, qk_norm_a, rpb_b, w_decay_c, b_decay_c, onorm_c,
                     w_br_a, w_br_b, w_br_c, w_out, w_ffn1_in, w_ffn1_out, w_ffn2_in, w_ffn2_out)
    return (y_prompt, y_sample)
```

```python
import functools

import jax
import jax.numpy as jnp
import numpy as np
from jax import lax
from jax.experimental import pallas as pl
from jax.experimental.pallas import tpu as pltpu

F32 = jnp.float32
BF16 = jnp.bfloat16

D_MODEL = 2048
SEQ = 2048
DEPTH = 4
GRID_W = 64
HEAD_DIM = 128
A_Q = 1024
A_KV = 256
A_GROUP = 4
A_KV_HEADS = 2
ROPE_THETA = 10000.0
B_W = 512
B_HEADS = 4
NA_ROWS = 8
NA_COLS = 16
C_V = 512
C_K = 256
C_DK = 64
C_RANK = 16
C_TAU = 16.0
D_FF = 5632
EPS = 1e-6

COL_AQ, COL_AK, COL_AV = 0, 1024, 1280
COL_BQ, COL_BK, COL_BV = 1536, 2048, 2560
COL_CQ, COL_CK, COL_CV, COL_COG = 3072, 3328, 3584, 4096
COL_CL = 4608
COL_GA, COL_GB, COL_GC = 5120, 7168, 9216
N_PROJ = 11264

NEG = -1e30
VMEM_LIMIT = 56 * 1024 * 1024

FFN_TM, FFN_TF = 512, 512
PROJ_TM, PROJ_TN = 512, 1024
MERGE_TM, MERGE_TN = 512, 512
ATT_TQ = 256
NB_QROWS = 8
NB_KROWS = 16
NB_TQ = NB_QROWS * GRID_W
NB_TK = NB_KROWS * GRID_W
GLA_C = 128
GLA_S = 16


def _rms(x, gain):
    ms = jnp.mean(x * x, axis=-1, keepdims=True)
    return x * lax.rsqrt(ms + EPS) * gain


def _sigmoid(x):
    return 1.0 / (1.0 + jnp.exp(-x))


def _dot_nt(a, b):
    return lax.dot_general(a, b, (((1,), (1,)), ((), ())), preferred_element_type=F32)


def _dot_tn(a, b):
    return lax.dot_general(a, b, (((0,), (0,)), ((), ())), preferred_element_type=F32)


def _params(*sem):
    return pltpu.CompilerParams(dimension_semantics=sem, vmem_limit_bytes=VMEM_LIMIT)


def _ffn_kernel(x_ref, ng_ref, wg_ref, wu_ref, wo_ref, o_ref, xn_ref, *, row_in, row_out):
    f = pl.program_id(1)

    @pl.when(f == 0)
    def _():
        xn_ref[...] = _rms(x_ref[...], ng_ref[row_in:row_in + 1, :]).astype(BF16)
        o_ref[...] = jnp.zeros_like(o_ref)

    xn = xn_ref[...]
    g = jnp.dot(xn, wg_ref[...], preferred_element_type=F32)
    u = jnp.dot(xn, wu_ref[...], preferred_element_type=F32)
    a = (g * _sigmoid(g) * u).astype(BF16)
    o_ref[...] += jnp.dot(a, wo_ref[...], preferred_element_type=F32)

    @pl.when(f == pl.num_programs(1) - 1)
    def _():
        y = _rms(o_ref[...], ng_ref[row_out:row_out + 1, :])
        o_ref[...] = x_ref[...] + 0.5 * y


def _ffn(x, ng, w_in, w_out, row_in, row_out):
    t = x.shape[0]
    nf = D_FF // FFN_TF
    return pl.pallas_call(
        functools.partial(_ffn_kernel, row_in=row_in, row_out=row_out),
        grid=(t // FFN_TM, nf),
        in_specs=[
            pl.BlockSpec((FFN_TM, D_MODEL), lambda i, f: (i, 0)),
            pl.BlockSpec((6, D_MODEL), lambda i, f: (0, 0)),
            pl.BlockSpec((D_MODEL, FFN_TF), lambda i, f: (0, f)),
            pl.BlockSpec((D_MODEL, FFN_TF), lambda i, f: (0, f + nf)),
            pl.BlockSpec((FFN_TF, D_MODEL), lambda i, f: (f, 0)),
        ],
        out_specs=pl.BlockSpec((FFN_TM, D_MODEL), lambda i, f: (i, 0)),
        out_shape=jax.ShapeDtypeStruct((t, D_MODEL), F32),
        scratch_shapes=[pltpu.VMEM((FFN_TM, D_MODEL), BF16)],
        compiler_params=_params("parallel", "arbitrary"),
        name="ffn",
    )(x, ng, w_in, w_in, w_out)


def _proj_kernel(x_ref, ng_ref, w_ref, o_ref, xn_ref):
    @pl.when(pl.program_id(1) == 0)
    def _():
        xn_ref[...] = _rms(x_ref[...], ng_ref[2:3, :]).astype(BF16)

    o_ref[...] = jnp.dot(xn_ref[...], w_ref[...], preferred_element_type=F32).astype(BF16)


def _proj(x, ng, w):
    t = x.shape[0]
    return pl.pallas_call(
        _proj_kernel,
        grid=(t // PROJ_TM, N_PROJ // PROJ_TN),
        in_specs=[
            pl.BlockSpec((PROJ_TM, D_MODEL), lambda i, j: (i, 0)),
            pl.BlockSpec((6, D_MODEL), lambda i, j: (0, 0)),
            pl.BlockSpec((D_MODEL, PROJ_TN), lambda i, j: (0, j)),
        ],
        out_specs=pl.BlockSpec((PROJ_TM, PROJ_TN), lambda i, j: (i, j)),
        out_shape=jax.ShapeDtypeStruct((t, N_PROJ), BF16),
        scratch_shapes=[pltpu.VMEM((PROJ_TM, D_MODEL), BF16)],
        compiler_params=_params("parallel", "arbitrary"),
        name="proj",
    )(x, ng, w)


def _norm_rope(x, gain, cos, sin_signed, even_lane):
    y = _rms(x, gain)
    swapped = jnp.where(even_lane, pltpu.roll(y, HEAD_DIM - 1, 1), pltpu.roll(y, 1, 1))
    return y * cos + swapped * sin_signed


def _attn_a_kernel(q_ref, k_ref, v_ref, cq_ref, sq_ref, ck_ref, sk_ref, gain_ref, o_ref, kn_ref):
    @pl.when(pl.program_id(2) == 0)
    def _():
        even_k = (lax.broadcasted_iota(jnp.int32, (SEQ, HEAD_DIM), 1) & 1) == 0
        k = k_ref[...].astype(F32)
        kn_ref[...] = _norm_rope(k, gain_ref[1:2, :], ck_ref[...], sk_ref[...], even_k).astype(BF16)

    even_q = (lax.broadcasted_iota(jnp.int32, (ATT_TQ, HEAD_DIM), 1) & 1) == 0
    kn = kn_ref[...]
    v = v_ref[...]
    cq = cq_ref[...]
    sq = sq_ref[...]
    scale = HEAD_DIM ** -0.5
    for h in range(A_GROUP):
        q = q_ref[:, h * HEAD_DIM:(h + 1) * HEAD_DIM].astype(F32)
        qn = (_norm_rope(q, gain_ref[0:1, :], cq, sq, even_q) * scale).astype(BF16)
        s = _dot_nt(qn, kn)
        m = jnp.max(s, axis=-1, keepdims=True)
        p = jnp.exp(s - m)
        l = jnp.sum(p, axis=-1, keepdims=True)
        o = jnp.dot(p.astype(BF16), v, preferred_element_type=F32) / l
        o_ref[:, h * HEAD_DIM:(h + 1) * HEAD_DIM] = o.astype(BF16)


def _attn_a(proj, bsz, cos, sin_signed, qk_gain):
    t = proj.shape[0]
    nq = SEQ // ATT_TQ
    qw = A_GROUP * HEAD_DIM
    return pl.pallas_call(
        _attn_a_kernel,
        grid=(bsz, A_KV_HEADS, nq),
        in_specs=[
            pl.BlockSpec((ATT_TQ, qw), lambda b, g, i: (b * nq + i, COL_AQ // qw + g)),
            pl.BlockSpec((SEQ, HEAD_DIM), lambda b, g, i: (b, COL_AK // HEAD_DIM + g)),
            pl.BlockSpec((SEQ, HEAD_DIM), lambda b, g, i: (b, COL_AV // HEAD_DIM + g)),
            pl.BlockSpec((ATT_TQ, HEAD_DIM), lambda b, g, i: (i, 0)),
            pl.BlockSpec((ATT_TQ, HEAD_DIM), lambda b, g, i: (i, 0)),
            pl.BlockSpec((SEQ, HEAD_DIM), lambda b, g, i: (0, 0)),
            pl.BlockSpec((SEQ, HEAD_DIM), lambda b, g, i: (0, 0)),
            pl.BlockSpec((2, HEAD_DIM), lambda b, g, i: (0, 0)),
        ],
        out_specs=pl.BlockSpec((ATT_TQ, qw), lambda b, g, i: (b * nq + i, g)),
        out_shape=jax.ShapeDtypeStruct((t, A_Q), BF16),
        scratch_shapes=[pltpu.VMEM((SEQ, HEAD_DIM), BF16)],
        compiler_params=_params("parallel", "arbitrary", "arbitrary"),
        name="attn_a",
    )(proj, proj, proj, cos, sin_signed, cos, sin_signed, qk_gain)


def _nb_key_row_start(r):
    rows = SEQ // GRID_W
    return int(np.clip(r * NB_QROWS - NA_ROWS // 2, 0, rows - NB_KROWS))


def _attn_b_kernel(q_ref, k_ref, v_ref, bias_ref, o_ref):
    scale = HEAD_DIM ** -0.5
    for r in range(SEQ // NB_TQ):
        ks = _nb_key_row_start(r) * GRID_W
        q = (q_ref[r * NB_TQ:(r + 1) * NB_TQ, :].astype(F32) * scale).astype(BF16)
        k = k_ref[ks:ks + NB_TK, :]
        v = v_ref[ks:ks + NB_TK, :]
        s = _dot_nt(q, k) + bias_ref[0, r]
        m = jnp.max(s, axis=-1, keepdims=True)
        p = jnp.exp(s - m)
        l = jnp.sum(p, axis=-1, keepdims=True)
        o = jnp.dot(p.astype(BF16), v, preferred_element_type=F32) / l
        o_ref[r * NB_TQ:(r + 1) * NB_TQ, :] = o.astype(BF16)


def _attn_b(proj, bsz, bias):
    t = proj.shape[0]
    nr = SEQ // NB_TQ
    return pl.pallas_call(
        _attn_b_kernel,
        grid=(B_HEADS, bsz),
        in_specs=[
            pl.BlockSpec((SEQ, HEAD_DIM), lambda h, b: (b, COL_BQ // HEAD_DIM + h)),
            pl.BlockSpec((SEQ, HEAD_DIM), lambda h, b: (b, COL_BK // HEAD_DIM + h)),
            pl.BlockSpec((SEQ, HEAD_DIM), lambda h, b: (b, COL_BV // HEAD_DIM + h)),
            pl.BlockSpec((1, nr, NB_TQ, NB_TK), lambda h, b: (h, 0, 0, 0)),
        ],
        out_specs=pl.BlockSpec((SEQ, HEAD_DIM), lambda h, b: (b, h)),
        out_shape=jax.ShapeDtypeStruct((t, B_W), BF16),
        compiler_params=_params("arbitrary", "arbitrary"),
        name="attn_b",
    )(proj, proj, proj, bias)


def _nb_bias_tables(rpb):
    rows = SEQ // GRID_W
    nr = SEQ // NB_TQ
    n_dr, n_dc = 2 * NA_ROWS - 1, 2 * NA_COLS - 1
    row_sel = np.zeros((nr, NB_QROWS, NB_KROWS, n_dr), np.float32)
    for r in range(nr):
        for rq in range(NB_QROWS):
            gq = r * NB_QROWS + rq
            rs = int(np.clip(gq - NA_ROWS // 2, 0, rows - NA_ROWS))
            for rk in range(NB_KROWS):
                gk = _nb_key_row_start(r) + rk
                if rs <= gk < rs + NA_ROWS:
                    row_sel[r, rq, rk, gk - gq + NA_ROWS - 1] = 1.0
    col_sel = np.zeros((GRID_W, GRID_W, n_dc), np.float32)
    for cq in range(GRID_W):
        cs = int(np.clip(cq - NA_COLS // 2, 0, GRID_W - NA_COLS))
        for ck in range(cs, cs + NA_COLS):
            col_sel[cq, ck, ck - cq + NA_COLS - 1] = 1.0
    valid = np.einsum("rqkd,xye->rqxky", row_sel, col_sel) > 0.5
    t1 = jnp.einsum("rqkd,lhde->lhrqke", jnp.asarray(row_sel), rpb, precision=lax.Precision.HIGHEST)
    tab = jnp.einsum("lhrqke,xye->lhrqxky", t1, jnp.asarray(col_sel), precision=lax.Precision.HIGHEST)
    tab = jnp.where(jnp.asarray(valid)[None, None], tab, NEG)
    return tab.reshape(rpb.shape[0], B_HEADS, nr, NB_TQ, NB_TK)


def _gla_kernel(q_ref, k_ref, v_ref, og_ref, lr_ref, wd_ref, bd_ref, onorm_ref, o_ref,
                of_ref, b_ref, kd_ref, qd_ref, ko_ref, qo_ref, st_ref):
    c_rows, s_rows = GLA_C, GLA_S
    nblk = c_rows // s_rows
    nch = SEQ // c_rows
    lanes = 2 * C_DK
    s_shift = s_rows.bit_length() - 1
    rev = pl.program_id(2)

    row = lax.broadcasted_iota(jnp.int32, (c_rows, lanes), 0)
    lane = lax.broadcasted_iota(jnp.int32, (c_rows, lanes), 1)
    tau = jnp.where(rev == 0, row, c_rows - 1 - row)
    pos = tau & (s_rows - 1)
    blk = tau >> s_shift
    head_of_lane = lane >> 6
    rr = lax.broadcasted_iota(jnp.int32, (c_rows, c_rows), 0)
    cc = lax.broadcasted_iota(jnp.int32, (c_rows, c_rows), 1)
    same_blk = (rr >> s_shift) == (cc >> s_shift)
    tau_r = jnp.where(rev == 0, rr, c_rows - 1 - rr)
    tau_c = jnp.where(rev == 0, cc, c_rows - 1 - cc)
    tri_bf = jnp.where(tau_c <= tau_r, 1.0, 0.0).astype(BF16)
    row_last = jnp.where(rev == 0, c_rows - 1, 0)

    st_ref[...] = jnp.zeros_like(st_ref)

    def chunk(ci, carry):
        c_phys = jnp.where(rev == 0, ci, nch - 1 - ci)
        rows = pl.ds(pl.multiple_of(c_phys * c_rows, c_rows), c_rows)
        qf = q_ref[rows, :].astype(F32) * (C_DK ** -0.5)
        kf = k_ref[rows, :].astype(F32)
        z = jnp.dot(lr_ref[rows, :], wd_ref[0], preferred_element_type=F32) + bd_ref[0]
        g = (jnp.minimum(z, 0.0) - jnp.log(1.0 + jnp.exp(-jnp.abs(z)))) * (1.0 / C_TAU)
        g_hi = g.astype(BF16)
        g_lo = (g - g_hi.astype(F32)).astype(BF16)
        b = (jnp.dot(tri_bf, g_hi, preferred_element_type=F32)
             + jnp.dot(tri_bf, g_lo, preferred_element_type=F32))
        b_ref[...] = b

        q_head = [jnp.where(head_of_lane == h, qf, 0.0).astype(BF16) for h in range(2)]

        for i in range(s_rows):
            off = jnp.where(rev == 0, i, s_rows - 1 - i)
            ref = jnp.concatenate(
                [jnp.broadcast_to(b_ref[pl.ds(pb * s_rows + off, 1), :], (s_rows, lanes))
                 for pb in range(nblk)], axis=0)
            e = jnp.where(pos <= i, ref - b, NEG)
            kd_ref[:, i * lanes:(i + 1) * lanes] = (kf * jnp.exp(e)).astype(BF16)
            for h in range(2):
                qd_ref[h, :, i * lanes:(i + 1) * lanes] = jnp.where(pos == i, q_head[h], 0.0)

        for ib in range(1, nblk):
            r_row = jnp.where(rev == 0, ib * s_rows - 1, c_rows - ib * s_rows)
            r = jnp.broadcast_to(b_ref[pl.ds(r_row, 1), :], (c_rows, lanes))
            ek = jnp.where(blk < ib, r - b, NEG)
            ko_ref[:, (ib - 1) * lanes:ib * lanes] = (kf * jnp.exp(ek)).astype(BF16)
            eq = jnp.where(blk == ib, b - r, NEG)
            qo = qf * jnp.exp(eq)
            for h in range(2):
                qo_ref[h, :, (ib - 1) * lanes:ib * lanes] = (
                    jnp.where(head_of_lane == h, qo, 0.0).astype(BF16))

        b_last = b_ref[pl.ds(row_last, 1), :]
        k_tail = (kf * jnp.exp(b_last - b)).astype(BF16)
        q_in = qf * jnp.exp(b)
        decay_all = jnp.exp(b_last)

        for h in range(2):
            v_h = v_ref[rows, h * HEAD_DIM:(h + 1) * HEAD_DIM]
            a = jnp.where(same_blk, _dot_nt(qd_ref[h], kd_ref[...]), 0.0)
            a = a + _dot_nt(qo_ref[h], ko_ref[...])
            st = st_ref[h]
            q_in_h = jnp.where(head_of_lane == h, q_in, 0.0).astype(BF16)
            o_h = (jnp.dot(a.astype(BF16), v_h, preferred_element_type=F32)
                   + _dot_nt(q_in_h, st.astype(BF16)))
            st_ref[h] = st * decay_all + _dot_tn(v_h, k_tail)

            @pl.when(rev == 0)
            def _():
                of_ref[rows, h * HEAD_DIM:(h + 1) * HEAD_DIM] = o_h

            @pl.when(rev == 1)
            def _():
                tot = o_h + of_ref[rows, h * HEAD_DIM:(h + 1) * HEAD_DIM]
                y = _rms(tot, onorm_ref[...])
                gate = og_ref[rows, h * HEAD_DIM:(h + 1) * HEAD_DIM].astype(F32)
                o_ref[rows, h * HEAD_DIM:(h + 1) * HEAD_DIM] = (y * gate * _sigmoid(gate)).astype(BF16)

        return carry

    lax.fori_loop(0, nch, chunk, 0)


def _gla(proj, bsz, w_dec, b_dec, onorm):
    t = proj.shape[0]
    lanes = 2 * C_DK
    vw = 2 * HEAD_DIM
    nblk = GLA_C // GLA_S
    return pl.pallas_call(
        _gla_kernel,
        grid=(bsz, 2, 2),
        in_specs=[
            pl.BlockSpec((SEQ, lanes), lambda b, p, d: (b, COL_CQ // lanes + p)),
            pl.BlockSpec((SEQ, lanes), lambda b, p, d: (b, COL_CK // lanes + p)),
            pl.BlockSpec((SEQ, vw), lambda b, p, d: (b, COL_CV // vw + p)),
            pl.BlockSpec((SEQ, vw), lambda b, p, d: (b, COL_COG // vw + p)),
            pl.BlockSpec((SEQ, lanes), lambda b, p, d: (b, COL_CL // lanes)),
            pl.BlockSpec((1, lanes, lanes), lambda b, p, d: (d, 0, p)),
            pl.BlockSpec((1, 1, lanes), lambda b, p, d: (d, 0, p)),
            pl.BlockSpec((1, HEAD_DIM), lambda b, p, d: (0, 0)),
        ],
        out_specs=pl.BlockSpec((SEQ, vw), lambda b, p, d: (b, p)),
        out_shape=jax.ShapeDtypeStruct((t, C_V), BF16),
        scratch_shapes=[
            pltpu.VMEM((SEQ, vw), F32),
            pltpu.VMEM((GLA_C, lanes), F32),
            pltpu.VMEM((GLA_C, GLA_S * lanes), BF16),
            pltpu.VMEM((2, GLA_C, GLA_S * lanes), BF16),
            pltpu.VMEM((GLA_C, (nblk - 1) * lanes), BF16),
            pltpu.VMEM((2, GLA_C, (nblk - 1) * lanes), BF16),
            pltpu.VMEM((2, HEAD_DIM, lanes), F32),
        ],
        compiler_params=_params("parallel", "arbitrary", "arbitrary"),
        name="gla",
    )(proj, proj, proj, proj, proj, w_dec, b_dec, onorm)


def _merge_kernel(x_ref, oa_ref, ob_ref, oc_ref, ga_ref, gb_ref, gc_ref, gbias_ref,
                  wa_ref, wb_ref, wc_ref, wo_ref, ng_ref, o_ref):
    j = pl.program_id(1)

    @pl.when(j == 0)
    def _():
        o_ref[...] = jnp.zeros_like(o_ref)

    def branch(o_b, w_b, g_b, row):
        y = jnp.dot(o_b[...], w_b[...], preferred_element_type=F32)
        return _sigmoid(g_b[...].astype(F32) + gbias_ref[row:row + 1, :]) * y

    merged = (branch(oa_ref, wa_ref, ga_ref, 0) + branch(ob_ref, wb_ref, gb_ref, 1)
              + branch(oc_ref, wc_ref, gc_ref, 2))
    o_ref[...] += jnp.dot(merged.astype(BF16), wo_ref[...], preferred_element_type=F32)

    @pl.when(j == pl.num_programs(1) - 1)
    def _():
        o_ref[...] = x_ref[...] + _rms(o_ref[...], ng_ref[3:4, :])


def _merge(x, oa, ob, oc, proj, gate_bias, wa, wb, wc, wo, ng):
    t = x.shape[0]
    tm, tn = MERGE_TM, MERGE_TN
    return pl.pallas_call(
        _merge_kernel,
        grid=(t // tm, D_MODEL // tn),
        in_specs=[
            pl.BlockSpec((tm, D_MODEL), lambda i, j: (i, 0)),
            pl.BlockSpec((tm, A_Q), lambda i, j: (i, 0)),
            pl.BlockSpec((tm, B_W), lambda i, j: (i, 0)),
            pl.BlockSpec((tm, C_V), lambda i, j: (i, 0)),
            pl.BlockSpec((tm, tn), lambda i, j: (i, COL_GA // tn + j)),
            pl.BlockSpec((tm, tn), lambda i, j: (i, COL_GB // tn + j)),
            pl.BlockSpec((tm, tn), lambda i, j: (i, COL_GC // tn + j)),
            pl.BlockSpec((3, tn), lambda i, j: (0, j)),
            pl.BlockSpec((A_Q, tn), lambda i, j: (0, j)),
            pl.BlockSpec((B_W, tn), lambda i, j: (0, j)),
            pl.BlockSpec((C_V, tn), lambda i, j: (0, j)),
            pl.BlockSpec((tn, D_MODEL), lambda i, j: (j, 0)),
            pl.BlockSpec((6, D_MODEL), lambda i, j: (0, 0)),
        ],
        out_specs=pl.BlockSpec((tm, D_MODEL), lambda i, j: (i, 0)),
        out_shape=jax.ShapeDtypeStruct((t, D_MODEL), F32),
        compiler_params=_params("parallel", "arbitrary"),
        name="merge",
    )(x, oa, ob, oc, proj, proj, proj, gate_bias, wa, wb, wc, wo, ng)


def _rope_tables():
    t = jnp.arange(SEQ)
    half = HEAD_DIM // 2
    inv = ROPE_THETA ** (-jnp.arange(0, half, 2, dtype=F32) / half)
    pos_r = (t // GRID_W).astype(F32)
    pos_c = (t % GRID_W).astype(F32)
    ang = jnp.concatenate([pos_r[:, None] * inv, pos_c[:, None] * inv], axis=-1)
    cos = jnp.repeat(jnp.cos(ang), 2, axis=-1)
    sign = jnp.asarray(np.tile(np.array([-1.0, 1.0], np.float32), half))
    sin_signed = jnp.repeat(jnp.sin(ang), 2, axis=-1) * sign
    return cos.astype(F32), sin_signed.astype(F32)


def _relayout_w_in(w_in):
    n_head = COL_CL + 2 * C_RANK
    pad = jnp.zeros(w_in.shape[:2] + (COL_GA - n_head,), w_in.dtype)
    return jnp.concatenate([w_in[..., :n_head], pad, w_in[..., n_head:]], axis=-1).astype(BF16)


def _decay_weights(w_decay, b_decay):
    lanes = 2 * C_DK
    w = jnp.zeros((w_decay.shape[0], 2, lanes, C_K), F32)
    w = w.at[:, 0, 0:C_RANK].set(w_decay[:, 0])
    w = w.at[:, 1, C_RANK:2 * C_RANK].set(w_decay[:, 1])
    return w.astype(BF16), b_decay[:, :, None, :].astype(F32)


def kernel(x_prompt, x_sample, norm_gains, w_in, gate_bias, qk_norm_a, rpb_b, w_decay_c, b_decay_c,
           onorm_c, w_br_a, w_br_b, w_br_c, w_out, w_ffn1_in, w_ffn1_out, w_ffn2_in, w_ffn2_out):
    assert x_prompt.shape[1:] == (SEQ, D_MODEL) and x_sample.shape[1:] == (SEQ, D_MODEL)
    cos, sin_signed = _rope_tables()
    w_proj = _relayout_w_in(w_in)
    w_dec, b_dec = _decay_weights(w_decay_c, b_decay_c)
    nb_bias = _nb_bias_tables(rpb_b)
    wa, wb, wc, wo = (w.astype(BF16) for w in (w_br_a, w_br_b, w_br_c, w_out))
    f1i, f1o, f2i, f2o = (w.astype(BF16) for w in (w_ffn1_in, w_ffn1_out, w_ffn2_in, w_ffn2_out))
    onorm = onorm_c[:, None, :]

    def trunk(x):
        bsz = x.shape[0]
        x = x.reshape(bsz * SEQ, D_MODEL)
        for l in range(DEPTH):
            ng = norm_gains[l]
            x = _ffn(x, ng, f1i[l], f1o[l], 0, 1)
            proj = _proj(x, ng, w_proj[l])
            oa = _attn_a(proj, bsz, cos, sin_signed, qk_norm_a[l])
            ob = _attn_b(proj, bsz, nb_bias[l])
            oc = _gla(proj, bsz, w_dec[l], b_dec[l], onorm[l])
            x = _merge(x, oa, ob, oc, proj, gate_bias[l], wa[l], wb[l], wc[l], wo[l], ng)
            x = _ffn(x, ng, f2i[l], f2o[l], 4, 5)
        return x.reshape(bsz, SEQ, D_MODEL)

    return trunk(x_prompt), trunk(x_sample)
```

```python
import functools
import math

import jax
import jax.numpy as jnp
import numpy as np
from jax import lax
from jax.experimental import pallas as pl
from jax.experimental.pallas import tpu as pltpu

F32 = jnp.float32
BF16 = jnp.bfloat16

D_MODEL = 2048
SEQ = 2048
DEPTH = 4
GRID_W = 64
HEAD_DIM = 128
A_Q = 1024
A_KV = 256
A_GROUP = 4
A_KV_HEADS = 2
ROPE_THETA = 10000.0
B_W = 512
B_HEADS = 4
NA_ROWS = 8
NA_COLS = 16
C_V = 512
C_K = 256
C_DK = 64
C_RANK = 16
C_TAU = 16.0
D_FF = 5632
EPS = 1e-6

COL_AQ, COL_AK, COL_AV = 0, 1024, 1280
COL_BQ, COL_BK, COL_BV = 1536, 2048, 2560
COL_CQ, COL_CK, COL_CV, COL_COG = 3072, 3328, 3584, 4096
COL_CL = 4608
COL_GA, COL_GB, COL_GC = 5120, 7168, 9216
N_PROJ = 11264

NEG = -1e30
LOG2E = math.log2(math.e)
VMEM_LIMIT = 56 * 1024 * 1024

FFN_TM, FFN_TF = 1024, 256
PROJ_TM, PROJ_TN = 1024, 1024
MERGE_TM, MERGE_TN = 512, 512
ATT_TQ = 256
NB_QROWS = 8
NB_KROWS = 16
NB_TQ = NB_QROWS * GRID_W
NB_TK = NB_KROWS * GRID_W
GLA_C = 128
GLA_LEVELS = GLA_C.bit_length() - 1


def _rms(x, gain):
    ms = jnp.mean(x * x, axis=-1, keepdims=True)
    return x * lax.rsqrt(ms + EPS) * gain


def _sigmoid(x):
    return 1.0 / (1.0 + jnp.exp(-x))


def _dot_nt(a, b):
    return lax.dot_general(a, b, (((1,), (1,)), ((), ())), preferred_element_type=F32)


def _dot_tn(a, b):
    return lax.dot_general(a, b, (((0,), (0,)), ((), ())), preferred_element_type=F32)


def _params(*sem):
    return pltpu.CompilerParams(dimension_semantics=sem, vmem_limit_bytes=VMEM_LIMIT)


def _ffn_kernel(x_ref, ng_ref, wg_ref, wu_ref, wo_ref, o_ref, xn_ref, *, row_in, row_out):
    f = pl.program_id(1)

    @pl.when(f == 0)
    def _():
        xn_ref[...] = _rms(x_ref[...], ng_ref[row_in:row_in + 1, :]).astype(BF16)
        o_ref[...] = jnp.zeros_like(o_ref)

    xn = xn_ref[...]
    g = jnp.dot(xn, wg_ref[...], preferred_element_type=F32)
    u = jnp.dot(xn, wu_ref[...], preferred_element_type=F32)
    a = (g * _sigmoid(g) * u).astype(BF16)
    o_ref[...] += jnp.dot(a, wo_ref[...], preferred_element_type=F32)

    @pl.when(f == pl.num_programs(1) - 1)
    def _():
        y = _rms(o_ref[...], ng_ref[row_out:row_out + 1, :])
        o_ref[...] = x_ref[...] + 0.5 * y


def _ffn(x, ng, w_in, w_out, layer, row_in, row_out):
    t = x.shape[0]
    nf = D_FF // FFN_TF
    return pl.pallas_call(
        functools.partial(_ffn_kernel, row_in=row_in, row_out=row_out),
        grid=(t // FFN_TM, nf),
        in_specs=[
            pl.BlockSpec((FFN_TM, D_MODEL), lambda i, f: (i, 0)),
            pl.BlockSpec((None, 6, D_MODEL), lambda i, f: (layer, 0, 0)),
            pl.BlockSpec((None, D_MODEL, FFN_TF), lambda i, f: (layer, 0, f)),
            pl.BlockSpec((None, D_MODEL, FFN_TF), lambda i, f: (layer, 0, f + nf)),
            pl.BlockSpec((None, FFN_TF, D_MODEL), lambda i, f: (layer, f, 0)),
        ],
        out_specs=pl.BlockSpec((FFN_TM, D_MODEL), lambda i, f: (i, 0)),
        out_shape=jax.ShapeDtypeStruct((t, D_MODEL), F32),
        scratch_shapes=[pltpu.VMEM((FFN_TM, D_MODEL), BF16)],
        compiler_params=_params("parallel", "arbitrary"),
        name="ffn",
    )(x, ng, w_in, w_in, w_out)


def _proj_kernel(x_ref, ng_ref, w_ref, o_ref, xn_ref):
    @pl.when(pl.program_id(1) == 0)
    def _():
        xn_ref[...] = _rms(x_ref[...], ng_ref[2:3, :]).astype(BF16)

    o_ref[...] = jnp.dot(xn_ref[...], w_ref[...], preferred_element_type=F32).astype(BF16)


def _proj(x, ng, w, layer):
    t = x.shape[0]
    return pl.pallas_call(
        _proj_kernel,
        grid=(t // PROJ_TM, N_PROJ // PROJ_TN),
        in_specs=[
            pl.BlockSpec((PROJ_TM, D_MODEL), lambda i, j: (i, 0)),
            pl.BlockSpec((None, 6, D_MODEL), lambda i, j: (layer, 0, 0)),
            pl.BlockSpec((None, D_MODEL, PROJ_TN), lambda i, j: (layer, 0, j)),
        ],
        out_specs=pl.BlockSpec((PROJ_TM, PROJ_TN), lambda i, j: (i, j)),
        out_shape=jax.ShapeDtypeStruct((t, N_PROJ), BF16),
        scratch_shapes=[pltpu.VMEM((PROJ_TM, D_MODEL), BF16)],
        compiler_params=_params("parallel", "arbitrary"),
        name="proj",
    )(x, ng, w)


def _norm_rope(x, gain, cos, sin_signed, even_lane):
    y = _rms(x, gain)
    swapped = jnp.where(even_lane, pltpu.roll(y, HEAD_DIM - 1, 1), pltpu.roll(y, 1, 1))
    return y * cos + swapped * sin_signed


def _softmax_pv(s, v):
    m = jnp.max(s, axis=-1, keepdims=True)
    p = jnp.exp2(s - m)
    l = jnp.sum(p, axis=-1, keepdims=True)
    return jnp.dot(p.astype(BF16), v, preferred_element_type=F32) / l


def _attn_a_kernel(q_ref, k_ref, v_ref, cq_ref, sq_ref, ck_ref, sk_ref, gain_ref, o_ref, kn_ref):
    @pl.when(pl.program_id(2) == 0)
    def _():
        even_k = (lax.broadcasted_iota(jnp.int32, (SEQ, HEAD_DIM), 1) & 1) == 0
        k = k_ref[...].astype(F32)
        kn_ref[...] = _norm_rope(k, gain_ref[1:2, :], ck_ref[...], sk_ref[...], even_k).astype(BF16)

    even_q = (lax.broadcasted_iota(jnp.int32, (ATT_TQ, HEAD_DIM), 1) & 1) == 0
    kn = kn_ref[...]
    v = v_ref[...]
    cq = cq_ref[...]
    sq = sq_ref[...]
    scale = HEAD_DIM ** -0.5 * LOG2E
    for h in range(A_GROUP):
        q = q_ref[:, h * HEAD_DIM:(h + 1) * HEAD_DIM].astype(F32)
        qn = (_norm_rope(q, gain_ref[0:1, :], cq, sq, even_q) * scale).astype(BF16)
        o = _softmax_pv(_dot_nt(qn, kn), v)
        o_ref[:, h * HEAD_DIM:(h + 1) * HEAD_DIM] = o.astype(BF16)


def _attn_a(proj, bsz, cos, sin_signed, qk_gain, layer):
    t = proj.shape[0]
    nq = SEQ // ATT_TQ
    qw = A_GROUP * HEAD_DIM
    return pl.pallas_call(
        _attn_a_kernel,
        grid=(bsz, A_KV_HEADS, nq),
        in_specs=[
            pl.BlockSpec((ATT_TQ, qw), lambda b, g, i: (b * nq + i, COL_AQ // qw + g)),
            pl.BlockSpec((SEQ, HEAD_DIM), lambda b, g, i: (b, COL_AK // HEAD_DIM + g)),
            pl.BlockSpec((SEQ, HEAD_DIM), lambda b, g, i: (b, COL_AV // HEAD_DIM + g)),
            pl.BlockSpec((ATT_TQ, HEAD_DIM), lambda b, g, i: (i, 0)),
            pl.BlockSpec((ATT_TQ, HEAD_DIM), lambda b, g, i: (i, 0)),
            pl.BlockSpec((SEQ, HEAD_DIM), lambda b, g, i: (0, 0)),
            pl.BlockSpec((SEQ, HEAD_DIM), lambda b, g, i: (0, 0)),
            pl.BlockSpec((None, 2, HEAD_DIM), lambda b, g, i: (layer, 0, 0)),
        ],
        out_specs=pl.BlockSpec((ATT_TQ, qw), lambda b, g, i: (b * nq + i, g)),
        out_shape=jax.ShapeDtypeStruct((t, A_Q), BF16),
        scratch_shapes=[pltpu.VMEM((SEQ, HEAD_DIM), BF16)],
        compiler_params=_params("parallel", "arbitrary", "arbitrary"),
        name="attn_a",
    )(proj, proj, proj, cos, sin_signed, cos, sin_signed, qk_gain)


def _nb_key_row_start(r):
    rows = SEQ // GRID_W
    return int(np.clip(r * NB_QROWS - NA_ROWS // 2, 0, rows - NB_KROWS))


def _attn_b_kernel(q_ref, k_ref, v_ref, bias_ref, o_ref):
    scale = HEAD_DIM ** -0.5 * LOG2E
    for r in range(SEQ // NB_TQ):
        ks = _nb_key_row_start(r) * GRID_W
        q = (q_ref[r * NB_TQ:(r + 1) * NB_TQ, :].astype(F32) * scale).astype(BF16)
        s = _dot_nt(q, k_ref[ks:ks + NB_TK, :]) + bias_ref[0, r]
        o = _softmax_pv(s, v_ref[ks:ks + NB_TK, :])
        o_ref[r * NB_TQ:(r + 1) * NB_TQ, :] = o.astype(BF16)


def _attn_b(proj, bsz, bias, layer):
    t = proj.shape[0]
    nr = SEQ // NB_TQ
    return pl.pallas_call(
        _attn_b_kernel,
        grid=(B_HEADS, bsz),
        in_specs=[
            pl.BlockSpec((SEQ, HEAD_DIM), lambda h, b: (b, COL_BQ // HEAD_DIM + h)),
            pl.BlockSpec((SEQ, HEAD_DIM), lambda h, b: (b, COL_BK // HEAD_DIM + h)),
            pl.BlockSpec((SEQ, HEAD_DIM), lambda h, b: (b, COL_BV // HEAD_DIM + h)),
            pl.BlockSpec((None, 1, nr, NB_TQ, NB_TK), lambda h, b: (layer, h, 0, 0, 0)),
        ],
        out_specs=pl.BlockSpec((SEQ, HEAD_DIM), lambda h, b: (b, h)),
        out_shape=jax.ShapeDtypeStruct((t, B_W), BF16),
        compiler_params=_params("arbitrary", "arbitrary"),
        name="attn_b",
    )(proj, proj, proj, bias)


def _nb_bias_tables(rpb):
    rows = SEQ // GRID_W
    nr = SEQ // NB_TQ
    n_dr, n_dc = 2 * NA_ROWS - 1, 2 * NA_COLS - 1
    row_sel = np.zeros((nr, NB_QROWS, NB_KROWS, n_dr), np.float32)
    for r in range(nr):
        for rq in range(NB_QROWS):
            gq = r * NB_QROWS + rq
            rs = int(np.clip(gq - NA_ROWS // 2, 0, rows - NA_ROWS))
            for rk in range(NB_KROWS):
                gk = _nb_key_row_start(r) + rk
                if rs <= gk < rs + NA_ROWS:
                    row_sel[r, rq, rk, gk - gq + NA_ROWS - 1] = 1.0
    col_sel = np.zeros((GRID_W, GRID_W, n_dc), np.float32)
    for cq in range(GRID_W):
        cs = int(np.clip(cq - NA_COLS // 2, 0, GRID_W - NA_COLS))
        for ck in range(cs, cs + NA_COLS):
            col_sel[cq, ck, ck - cq + NA_COLS - 1] = 1.0
    valid = np.einsum("rqkd,xye->rqxky", row_sel, col_sel) > 0.5
    t1 = jnp.einsum("rqkd,lhde->lhrqke", jnp.asarray(row_sel), rpb, precision=lax.Precision.HIGHEST)
    tab = jnp.einsum("lhrqke,xye->lhrqxky", t1, jnp.asarray(col_sel), precision=lax.Precision.HIGHEST)
    tab = jnp.where(jnp.asarray(valid)[None, None], tab * LOG2E, NEG)
    return tab.reshape(rpb.shape[0], B_HEADS, nr, NB_TQ, NB_TK)


def _gla_scan_matrices():
    c, nl = GLA_C, GLA_LEVELS
    m = np.zeros((2, nl + 2, c, c), np.float32)
    for d in range(2):
        tau = np.arange(c) if d == 0 else c - 1 - np.arange(c)
        tx, ty = tau[:, None], tau[None, :]
        for l in range(nl):
            left_end = ((tx >> (l + 1)) << (l + 1)) + (1 << l) - 1
            right = ((tx >> l) & 1) == 1
            m[d, l] = np.where(right, (ty > left_end) & (ty <= tx), (ty > tx) & (ty <= left_end))
        m[d, nl] = ty <= tx
        m[d, nl + 1] = ty > tx
    return jnp.asarray(m.reshape(2, (nl + 2) * c, c), BF16)


def _gla_kernel(q_ref, k_ref, v_ref, og_ref, lr_ref, wd_ref, bd_ref, onorm_ref, m_ref, o_ref,
                acc_ref, e_ref, st_ref):
    c, nl = GLA_C, GLA_LEVELS
    nch = SEQ // c
    lanes = 2 * C_DK

    row = lax.broadcasted_iota(jnp.int32, (c, lanes), 0)
    head_of_lane = lax.broadcasted_iota(jnp.int32, (c, lanes), 1) >> 6
    xr = ((lax.broadcasted_iota(jnp.int32, (2 * c, c), 0) & (c - 1))
          ^ lax.broadcasted_iota(jnp.int32, (2 * c, c), 1))

    st_ref[...] = jnp.zeros_like(st_ref)

    def stack_heads(per_head):
        return jnp.concatenate([x.astype(BF16) for x in per_head], axis=0)

    def scan_chunk(d, ci):
        c_phys = ci if d == 0 else nch - 1 - ci
        rows = pl.ds(pl.multiple_of(c_phys * c, c), c)
        qf = q_ref[rows, :].astype(F32) * (C_DK ** -0.5)
        kf = k_ref[rows, :].astype(F32)
        z = jnp.dot(lr_ref[rows, :], wd_ref[d], preferred_element_type=F32) + bd_ref[d]
        g = (jnp.minimum(z, 0.0) - jnp.log(1.0 + jnp.exp(-jnp.abs(z)))) * (1.0 / C_TAU)
        g_hi = g.astype(BF16)
        g_lo = (g - g_hi.astype(F32)).astype(BF16)
        e_ref[d] = (jnp.dot(m_ref[d], g_hi, preferred_element_type=F32)
                    + jnp.dot(m_ref[d], g_lo, preferred_element_type=F32))
        yield

        q_head = [jnp.where(head_of_lane == h, qf, 0.0) for h in range(2)]
        a = jnp.where(xr == 0, _dot_nt(stack_heads(q_head), kf.astype(BF16)), 0.0)
        yield
        for l in range(nl):
            t = jnp.exp(e_ref[d, l * c:(l + 1) * c, :])
            is_query_row = ((row >> l) & 1) == (1 - d)
            k_l = jnp.where(is_query_row, 0.0, kf * t).astype(BF16)
            q_l = stack_heads([jnp.where(is_query_row, q * t, 0.0) for q in q_head])
            p = _dot_nt(q_l, k_l)
            a = a + (jnp.where(xr < (2 << l), p, 0.0) if l < nl - 1 else p)
            yield

        t_in = jnp.exp(e_ref[d, nl * c:(nl + 1) * c, :])
        t_out = jnp.exp(e_ref[d, (nl + 1) * c:(nl + 2) * c, :])
        k_out = (kf * t_out).astype(BF16)
        q_in = stack_heads([q * t_in for q in q_head])
        a_bf = a.astype(BF16)
        last = c - 1 if d == 0 else 0
        decay_all = t_in[last:last + 1, :]
        yield
        for h in range(2):
            cols = slice(h * HEAD_DIM, (h + 1) * HEAD_DIM)
            v_h = v_ref[rows, cols]
            st = st_ref[d, h]
            acc_ref[d, rows, cols] = (
                jnp.dot(a_bf[h * c:(h + 1) * c], v_h, preferred_element_type=F32)
                + _dot_nt(q_in[h * c:(h + 1) * c], st.astype(BF16)))
            st_ref[d, h] = st * decay_all + _dot_tn(v_h, k_out)
            yield

    def scan_step(ci, carry):
        scans = [scan_chunk(0, ci), scan_chunk(1, ci)]
        while scans:
            scans = [s for s in scans if next(s, "done") is None]
        return carry

    lax.fori_loop(0, nch, scan_step, 0)

    def finish(ci, carry):
        rows = pl.ds(pl.multiple_of(ci * c, c), c)
        for h in range(2):
            cols = slice(h * HEAD_DIM, (h + 1) * HEAD_DIM)
            y = _rms(acc_ref[0, rows, cols] + acc_ref[1, rows, cols], onorm_ref[...])
            gate = og_ref[rows, cols].astype(F32)
            o_ref[rows, cols] = (y * gate * _sigmoid(gate)).astype(BF16)
        return carry

    lax.fori_loop(0, nch, finish, 0)


def _gla(proj, bsz, w_dec, b_dec, onorm, scan_m, layer):
    t = proj.shape[0]
    lanes = 2 * C_DK
    vw = 2 * HEAD_DIM
    n_e = (GLA_LEVELS + 2) * GLA_C
    return pl.pallas_call(
        _gla_kernel,
        grid=(bsz, 2),
        in_specs=[
            pl.BlockSpec((SEQ, lanes), lambda b, p: (b, COL_CQ // lanes + p)),
            pl.BlockSpec((SEQ, lanes), lambda b, p: (b, COL_CK // lanes + p)),
            pl.BlockSpec((SEQ, vw), lambda b, p: (b, COL_CV // vw + p)),
            pl.BlockSpec((SEQ, vw), lambda b, p: (b, COL_COG // vw + p)),
            pl.BlockSpec((SEQ, lanes), lambda b, p: (b, COL_CL // lanes)),
            pl.BlockSpec((None, 2, lanes, lanes), lambda b, p: (layer, 0, 0, p)),
            pl.BlockSpec((None, 2, 1, lanes), lambda b, p: (layer, 0, 0, p)),
            pl.BlockSpec((None, 1, HEAD_DIM), lambda b, p: (layer, 0, 0)),
            pl.BlockSpec((2, n_e, GLA_C), lambda b, p: (0, 0, 0)),
        ],
        out_specs=pl.BlockSpec((SEQ, vw), lambda b, p: (b, p)),
        out_shape=jax.ShapeDtypeStruct((t, C_V), BF16),
        scratch_shapes=[
            pltpu.VMEM((2, SEQ, vw), F32),
            pltpu.VMEM((2, n_e, lanes), F32),
            pltpu.VMEM((2, 2, HEAD_DIM, lanes), F32),
        ],
        compiler_params=_params("parallel", "arbitrary"),
        name="gla",
    )(proj, proj, proj, proj, proj, w_dec, b_dec, onorm, scan_m)


def _merge_kernel(x_ref, oa_ref, ob_ref, oc_ref, ga_ref, gb_ref, gc_ref, gbias_ref,
                  wa_ref, wb_ref, wc_ref, wo_ref, ng_ref, o_ref):
    j = pl.program_id(1)

    @pl.when(j == 0)
    def _():
        o_ref[...] = jnp.zeros_like(o_ref)

    def branch(o_b, w_b, g_b, row):
        y = jnp.dot(o_b[...], w_b[...], preferred_element_type=F32)
        return _sigmoid(g_b[...].astype(F32) + gbias_ref[row:row + 1, :]) * y

    merged = (branch(oa_ref, wa_ref, ga_ref, 0) + branch(ob_ref, wb_ref, gb_ref, 1)
              + branch(oc_ref, wc_ref, gc_ref, 2))
    o_ref[...] += jnp.dot(merged.astype(BF16), wo_ref[...], preferred_element_type=F32)

    @pl.when(j == pl.num_programs(1) - 1)
    def _():
        o_ref[...] = x_ref[...] + _rms(o_ref[...], ng_ref[3:4, :])


def _merge(x, oa, ob, oc, proj, gate_bias, wa, wb, wc, wo, ng, layer):
    t = x.shape[0]
    tm, tn = MERGE_TM, MERGE_TN
    return pl.pallas_call(
        _merge_kernel,
        grid=(t // tm, D_MODEL // tn),
        in_specs=[
            pl.BlockSpec((tm, D_MODEL), lambda i, j: (i, 0)),
            pl.BlockSpec((tm, A_Q), lambda i, j: (i, 0)),
            pl.BlockSpec((tm, B_W), lambda i, j: (i, 0)),
            pl.BlockSpec((tm, C_V), lambda i, j: (i, 0)),
            pl.BlockSpec((tm, tn), lambda i, j: (i, COL_GA // tn + j)),
            pl.BlockSpec((tm, tn), lambda i, j: (i, COL_GB // tn + j)),
            pl.BlockSpec((tm, tn), lambda i, j: (i, COL_GC // tn + j)),
            pl.BlockSpec((None, 3, tn), lambda i, j: (layer, 0, j)),
            pl.BlockSpec((None, A_Q, tn), lambda i, j: (layer, 0, j)),
            pl.BlockSpec((None, B_W, tn), lambda i, j: (layer, 0, j)),
            pl.BlockSpec((None, C_V, tn), lambda i, j: (layer, 0, j)),
            pl.BlockSpec((None, tn, D_MODEL), lambda i, j: (layer, j, 0)),
            pl.BlockSpec((None, 6, D_MODEL), lambda i, j: (layer, 0, 0)),
        ],
        out_specs=pl.BlockSpec((tm, D_MODEL), lambda i, j: (i, 0)),
        out_shape=jax.ShapeDtypeStruct((t, D_MODEL), F32),
        compiler_params=_params("parallel", "arbitrary"),
        name="merge",
    )(x, oa, ob, oc, proj, proj, proj, gate_bias, wa, wb, wc, wo, ng)


def _rope_tables():
    t = jnp.arange(SEQ)
    half = HEAD_DIM // 2
    inv = ROPE_THETA ** (-jnp.arange(0, half, 2, dtype=F32) / half)
    pos_r = (t // GRID_W).astype(F32)
    pos_c = (t % GRID_W).astype(F32)
    ang = jnp.concatenate([pos_r[:, None] * inv, pos_c[:, None] * inv], axis=-1)
    cos = jnp.repeat(jnp.cos(ang), 2, axis=-1)
    sign = jnp.asarray(np.tile(np.array([-1.0, 1.0], np.float32), half))
    sin_signed = jnp.repeat(jnp.sin(ang), 2, axis=-1) * sign
    return cos.astype(F32), sin_signed.astype(F32)


def _relayout_w_in(w_in):
    n_head = COL_CL + 2 * C_RANK
    pad = jnp.zeros(w_in.shape[:2] + (COL_GA - n_head,), BF16)
    return jnp.concatenate([w_in[..., :n_head].astype(BF16), pad, w_in[..., n_head:].astype(BF16)],
                           axis=-1)


def _decay_weights(w_decay, b_decay):
    lanes = 2 * C_DK
    zeros = jnp.zeros((w_decay.shape[0], lanes - 2 * C_RANK, C_K), w_decay.dtype)
    zero_r = jnp.zeros((w_decay.shape[0], C_RANK, C_K), w_decay.dtype)
    w_f = jnp.concatenate([w_decay[:, 0], zero_r, zeros], axis=1)
    w_b = jnp.concatenate([zero_r, w_decay[:, 1], zeros], axis=1)
    return jnp.stack([w_f, w_b], axis=1).astype(BF16), b_decay[:, :, None, :].astype(F32)


def kernel(x_prompt, x_sample, norm_gains, w_in, gate_bias, qk_norm_a, rpb_b, w_decay_c, b_decay_c,
           onorm_c, w_br_a, w_br_b, w_br_c, w_out, w_ffn1_in, w_ffn1_out, w_ffn2_in, w_ffn2_out):
    assert x_prompt.shape[1:] == (SEQ, D_MODEL) and x_sample.shape[1:] == (SEQ, D_MODEL)
    cos, sin_signed = _rope_tables()
    w_proj = _relayout_w_in(w_in)
    w_dec, b_dec = _decay_weights(w_decay_c, b_decay_c)
    nb_bias = _nb_bias_tables(rpb_b)
    scan_m = _gla_scan_matrices()
    wa, wb, wc, wo = (w.astype(BF16) for w in (w_br_a, w_br_b, w_br_c, w_out))
    f1i, f1o, f2i, f2o = (w.astype(BF16) for w in (w_ffn1_in, w_ffn1_out, w_ffn2_in, w_ffn2_out))
    onorm = onorm_c[:, None, :]

    def trunk(x):
        bsz = x.shape[0]
        x = x.reshape(bsz * SEQ, D_MODEL)
        for l in range(DEPTH):
            x = _ffn(x, norm_gains, f1i, f1o, l, 0, 1)
            proj = _proj(x, norm_gains, w_proj, l)
            oa = _attn_a(proj, bsz, cos, sin_signed, qk_norm_a, l)
            ob = _attn_b(proj, bsz, nb_bias, l)
            oc = _gla(proj, bsz, w_dec, b_dec, onorm, scan_m, l)
            x = _merge(x, oa, ob, oc, proj, gate_bias, wa, wb, wc, wo, norm_gains, l)
            x = _ffn(x, norm_gains, f2i, f2o, l, 4, 5)
        return x.reshape(bsz, SEQ, D_MODEL)

    return trunk(x_prompt), trunk(x_sample)
```

```python
import functools
import math

import jax
import jax.numpy as jnp
import numpy as np
from jax import lax
from jax.experimental import pallas as pl
from jax.experimental.pallas import tpu as pltpu

F32 = jnp.float32
BF16 = jnp.bfloat16

D_MODEL = 2048
SEQ = 2048
DEPTH = 4
GRID_W = 64
HEAD_DIM = 128
A_Q = 1024
A_KV = 256
A_GROUP = 4
A_KV_HEADS = 2
ROPE_THETA = 10000.0
B_W = 512
B_HEADS = 4
NA_ROWS = 8
NA_COLS = 16
C_V = 512
C_K = 256
C_DK = 64
C_RANK = 16
C_TAU = 16.0
D_FF = 5632
EPS = 1e-6

COL_AQ, COL_AK, COL_AV = 0, 1024, 1280
COL_BQ, COL_BK, COL_BV = 1536, 2048, 2560
COL_CQ, COL_CK, COL_CV, COL_COG = 3072, 3328, 3584, 4096
COL_CL = 4608
COL_GA, COL_GB, COL_GC = 5120, 7168, 9216
N_PROJ = 11264

NEG = -1e30
LOG2E = math.log2(math.e)
VMEM_LIMIT = 56 * 1024 * 1024

FFN_TM, FFN_TF = 1024, 256
PROJ_TM, PROJ_TN = 1024, 1024
MERGE_TM, MERGE_TN = 1024, 256
MERGE_VMEM_LIMIT = 60 * 1024 * 1024
ATT_TQ = 256
NB_QROWS = 8
NB_KROWS = 16
NB_TQ = NB_QROWS * GRID_W
NB_TK = NB_KROWS * GRID_W
GLA_C = 128
GLA_LEVELS = GLA_C.bit_length() - 1


def _rms(x, gain):
    ms = jnp.mean(x * x, axis=-1, keepdims=True)
    return x * lax.rsqrt(ms + EPS) * gain


def _sigmoid(x):
    return 1.0 / (1.0 + jnp.exp(-x))


def _dot_nt(a, b):
    return lax.dot_general(a, b, (((1,), (1,)), ((), ())), preferred_element_type=F32)


def _dot_tn(a, b):
    return lax.dot_general(a, b, (((0,), (0,)), ((), ())), preferred_element_type=F32)


def _params(*sem, vmem_limit=VMEM_LIMIT):
    return pltpu.CompilerParams(dimension_semantics=sem, vmem_limit_bytes=vmem_limit)


def _ffn_kernel(x_ref, ng_ref, wg_ref, wu_ref, wo_ref, o_ref, xn_ref, *, row_in, row_out):
    f = pl.program_id(1)

    @pl.when(f == 0)
    def _():
        xn_ref[...] = _rms(x_ref[...], ng_ref[row_in:row_in + 1, :]).astype(BF16)
        o_ref[...] = jnp.zeros_like(o_ref)

    xn = xn_ref[...]
    g = jnp.dot(xn, wg_ref[...], preferred_element_type=F32)
    u = jnp.dot(xn, wu_ref[...], preferred_element_type=F32)
    a = (g * _sigmoid(g) * u).astype(BF16)
    o_ref[...] += jnp.dot(a, wo_ref[...], preferred_element_type=F32)

    @pl.when(f == pl.num_programs(1) - 1)
    def _():
        y = _rms(o_ref[...], ng_ref[row_out:row_out + 1, :])
        o_ref[...] = x_ref[...] + 0.5 * y


def _ffn(x, ng, w_in, w_out, layer, row_in, row_out):
    t = x.shape[0]
    nf = D_FF // FFN_TF
    return pl.pallas_call(
        functools.partial(_ffn_kernel, row_in=row_in, row_out=row_out),
        grid=(t // FFN_TM, nf),
        in_specs=[
            pl.BlockSpec((FFN_TM, D_MODEL), lambda i, f: (i, 0)),
            pl.BlockSpec((None, 6, D_MODEL), lambda i, f: (layer, 0, 0)),
            pl.BlockSpec((None, None, D_MODEL, FFN_TF), lambda i, f: (layer, f, 0, 0)),
            pl.BlockSpec((None, None, D_MODEL, FFN_TF), lambda i, f: (layer, f + nf, 0, 0)),
            pl.BlockSpec((None, FFN_TF, D_MODEL), lambda i, f: (layer, f, 0)),
        ],
        out_specs=pl.BlockSpec((FFN_TM, D_MODEL), lambda i, f: (i, 0)),
        out_shape=jax.ShapeDtypeStruct((t, D_MODEL), F32),
        scratch_shapes=[pltpu.VMEM((FFN_TM, D_MODEL), BF16)],
        compiler_params=_params("parallel", "arbitrary"),
        name="ffn",
    )(x, ng, w_in, w_in, w_out)


def _proj_kernel(x_ref, ng_ref, w_ref, o_ref, xn_ref):
    @pl.when(pl.program_id(1) == 0)
    def _():
        xn_ref[...] = _rms(x_ref[...], ng_ref[2:3, :]).astype(BF16)

    o_ref[...] = jnp.dot(xn_ref[...], w_ref[...], preferred_element_type=F32).astype(BF16)


def _proj(x, ng, w, layer):
    t = x.shape[0]
    return pl.pallas_call(
        _proj_kernel,
        grid=(t // PROJ_TM, N_PROJ // PROJ_TN),
        in_specs=[
            pl.BlockSpec((PROJ_TM, D_MODEL), lambda i, j: (i, 0)),
            pl.BlockSpec((None, 6, D_MODEL), lambda i, j: (layer, 0, 0)),
            pl.BlockSpec((None, D_MODEL, PROJ_TN), lambda i, j: (layer, 0, j)),
        ],
        out_specs=pl.BlockSpec((PROJ_TM, PROJ_TN), lambda i, j: (i, j)),
        out_shape=jax.ShapeDtypeStruct((t, N_PROJ), BF16),
        scratch_shapes=[pltpu.VMEM((PROJ_TM, D_MODEL), BF16)],
        compiler_params=_params("parallel", "arbitrary"),
        name="proj",
    )(x, ng, w)


def _norm_rope(x, gain, cos, sin_signed, even_lane):
    y = _rms(x, gain)
    swapped = jnp.where(even_lane, pltpu.roll(y, HEAD_DIM - 1, 1), pltpu.roll(y, 1, 1))
    return y * cos + swapped * sin_signed


def _with_ones(v_ref, v1_ref):
    v1_ref[:, :HEAD_DIM] = v_ref[...]
    v1_ref[:, HEAD_DIM:] = jnp.ones((v_ref.shape[0], HEAD_DIM), BF16)


def _softmax_pv(s, v1):
    m = jnp.max(s, axis=-1, keepdims=True)
    p = jnp.exp2(s - m).astype(BF16)
    ov = jnp.dot(p, v1, preferred_element_type=F32)
    return ov[:, :HEAD_DIM] / ov[:, HEAD_DIM:]


def _attn_a_kernel(q_ref, k_ref, v_ref, cq_ref, sq_ref, ck_ref, sk_ref, gain_ref, o_ref,
                   kn_ref, v1_ref):
    @pl.when(pl.program_id(2) == 0)
    def _():
        even_k = (lax.broadcasted_iota(jnp.int32, (SEQ, HEAD_DIM), 1) & 1) == 0
        k = k_ref[...].astype(F32)
        kn_ref[...] = _norm_rope(k, gain_ref[1:2, :], ck_ref[...], sk_ref[...], even_k).astype(BF16)
        _with_ones(v_ref, v1_ref)

    even_q = (lax.broadcasted_iota(jnp.int32, (ATT_TQ, HEAD_DIM), 1) & 1) == 0
    kn = kn_ref[...]
    v = v1_ref[...]
    cq = cq_ref[...]
    sq = sq_ref[...]
    scale = HEAD_DIM ** -0.5 * LOG2E

    def scores(h):
        q = q_ref[:, h * HEAD_DIM:(h + 1) * HEAD_DIM].astype(F32)
        qn = (_norm_rope(q, gain_ref[0:1, :], cq, sq, even_q) * scale).astype(BF16)
        return _dot_nt(qn, kn)

    s_next = scores(0)
    for h in range(A_GROUP):
        s = s_next
        if h + 1 < A_GROUP:
            s_next = scores(h + 1)
        o_ref[:, h * HEAD_DIM:(h + 1) * HEAD_DIM] = _softmax_pv(s, v).astype(BF16)


def _attn_a(proj, bsz, cos, sin_signed, qk_gain, layer):
    t = proj.shape[0]
    nq = SEQ // ATT_TQ
    qw = A_GROUP * HEAD_DIM
    return pl.pallas_call(
        _attn_a_kernel,
        grid=(bsz, A_KV_HEADS, nq),
        in_specs=[
            pl.BlockSpec((ATT_TQ, qw), lambda b, g, i: (b * nq + i, COL_AQ // qw + g)),
            pl.BlockSpec((SEQ, HEAD_DIM), lambda b, g, i: (b, COL_AK // HEAD_DIM + g)),
            pl.BlockSpec((SEQ, HEAD_DIM), lambda b, g, i: (b, COL_AV // HEAD_DIM + g)),
            pl.BlockSpec((ATT_TQ, HEAD_DIM), lambda b, g, i: (i, 0)),
            pl.BlockSpec((ATT_TQ, HEAD_DIM), lambda b, g, i: (i, 0)),
            pl.BlockSpec((SEQ, HEAD_DIM), lambda b, g, i: (0, 0)),
            pl.BlockSpec((SEQ, HEAD_DIM), lambda b, g, i: (0, 0)),
            pl.BlockSpec((None, 2, HEAD_DIM), lambda b, g, i: (layer, 0, 0)),
        ],
        out_specs=pl.BlockSpec((ATT_TQ, qw), lambda b, g, i: (b * nq + i, g)),
        out_shape=jax.ShapeDtypeStruct((t, A_Q), BF16),
        scratch_shapes=[pltpu.VMEM((SEQ, HEAD_DIM), BF16), pltpu.VMEM((SEQ, 2 * HEAD_DIM), BF16)],
        compiler_params=_params("parallel", "arbitrary", "arbitrary"),
        name="attn_a",
    )(proj, proj, proj, cos, sin_signed, cos, sin_signed, qk_gain)


def _nb_key_row_start(r):
    rows = SEQ // GRID_W
    return int(np.clip(r * NB_QROWS - NA_ROWS // 2, 0, rows - NB_KROWS))


def _attn_b_kernel(q_ref, k_ref, v_ref, bias_ref, o_ref, v1_ref):
    scale = HEAD_DIM ** -0.5 * LOG2E
    nr = SEQ // NB_TQ
    key_start = [_nb_key_row_start(r) * GRID_W for r in range(nr)]
    _with_ones(v_ref, v1_ref)

    def scores(r):
        q = (q_ref[r * NB_TQ:(r + 1) * NB_TQ, :].astype(F32) * scale).astype(BF16)
        return _dot_nt(q, k_ref[key_start[r]:key_start[r] + NB_TK, :]) + bias_ref[0, r]

    s_next = scores(0)
    for r in range(nr):
        s = s_next
        if r + 1 < nr:
            s_next = scores(r + 1)
        o = _softmax_pv(s, v1_ref[key_start[r]:key_start[r] + NB_TK, :])
        o_ref[r * NB_TQ:(r + 1) * NB_TQ, :] = o.astype(BF16)


def _attn_b(proj, bsz, bias, layer):
    t = proj.shape[0]
    nr = SEQ // NB_TQ
    return pl.pallas_call(
        _attn_b_kernel,
        grid=(B_HEADS, bsz),
        in_specs=[
            pl.BlockSpec((SEQ, HEAD_DIM), lambda h, b: (b, COL_BQ // HEAD_DIM + h)),
            pl.BlockSpec((SEQ, HEAD_DIM), lambda h, b: (b, COL_BK // HEAD_DIM + h)),
            pl.BlockSpec((SEQ, HEAD_DIM), lambda h, b: (b, COL_BV // HEAD_DIM + h)),
            pl.BlockSpec((None, 1, nr, NB_TQ, NB_TK), lambda h, b: (layer, h, 0, 0, 0)),
        ],
        out_specs=pl.BlockSpec((SEQ, HEAD_DIM), lambda h, b: (b, h)),
        out_shape=jax.ShapeDtypeStruct((t, B_W), BF16),
        scratch_shapes=[pltpu.VMEM((SEQ, 2 * HEAD_DIM), BF16)],
        compiler_params=_params("arbitrary", "arbitrary"),
        name="attn_b",
    )(proj, proj, proj, bias)


def _nb_bias_tables(rpb):
    rows = SEQ // GRID_W
    nr = SEQ // NB_TQ
    n_dr, n_dc = 2 * NA_ROWS - 1, 2 * NA_COLS - 1
    row_sel = np.zeros((nr, NB_QROWS, NB_KROWS, n_dr), np.float32)
    for r in range(nr):
        for rq in range(NB_QROWS):
            gq = r * NB_QROWS + rq
            rs = int(np.clip(gq - NA_ROWS // 2, 0, rows - NA_ROWS))
            for rk in range(NB_KROWS):
                gk = _nb_key_row_start(r) + rk
                if rs <= gk < rs + NA_ROWS:
                    row_sel[r, rq, rk, gk - gq + NA_ROWS - 1] = 1.0
    col_sel = np.zeros((GRID_W, GRID_W, n_dc), np.float32)
    for cq in range(GRID_W):
        cs = int(np.clip(cq - NA_COLS // 2, 0, GRID_W - NA_COLS))
        for ck in range(cs, cs + NA_COLS):
            col_sel[cq, ck, ck - cq + NA_COLS - 1] = 1.0
    valid = np.einsum("rqkd,xye->rqxky", row_sel, col_sel) > 0.5
    t1 = jnp.einsum("rqkd,lhde->lhrqke", jnp.asarray(row_sel), rpb, precision=lax.Precision.HIGHEST)
    tab = jnp.einsum("lhrqke,xye->lhrqxky", t1, jnp.asarray(col_sel), precision=lax.Precision.HIGHEST)
    tab = jnp.where(jnp.asarray(valid)[None, None], tab * LOG2E, NEG)
    return tab.reshape(rpb.shape[0], B_HEADS, nr, NB_TQ, NB_TK)


def _gla_scan_matrices():
    c, nl = GLA_C, GLA_LEVELS
    m = np.zeros((2, nl + 2, c, c), np.float32)
    for d in range(2):
        tau = np.arange(c) if d == 0 else c - 1 - np.arange(c)
        tx, ty = tau[:, None], tau[None, :]
        for l in range(nl):
            left_end = ((tx >> (l + 1)) << (l + 1)) + (1 << l) - 1
            right = ((tx >> l) & 1) == 1
            m[d, l] = np.where(right, (ty > left_end) & (ty <= tx), (ty > tx) & (ty <= left_end))
        m[d, nl] = ty <= tx
        m[d, nl + 1] = ty > tx
    return jnp.asarray(m.reshape(2, (nl + 2) * c, c), BF16)


def _gla_kernel(q_ref, k_ref, v_ref, og_ref, lr_ref, wd_ref, bd_ref, onorm_ref, m_ref, o_ref,
                acc_ref, e_ref, st_ref):
    c, nl = GLA_C, GLA_LEVELS
    nch = SEQ // c
    lanes = 2 * C_DK

    row = lax.broadcasted_iota(jnp.int32, (c, lanes), 0)
    head_of_lane = lax.broadcasted_iota(jnp.int32, (c, lanes), 1) >> 6
    xr = ((lax.broadcasted_iota(jnp.int32, (2 * c, c), 0) & (c - 1))
          ^ lax.broadcasted_iota(jnp.int32, (2 * c, c), 1))

    st_ref[...] = jnp.zeros_like(st_ref)

    def stack_heads(per_head):
        return jnp.concatenate([x.astype(BF16) for x in per_head], axis=0)

    def scan_chunk(pair, d, ci):
        c_phys = ci if d == 0 else nch - 1 - ci
        rows = pl.ds(pl.multiple_of(c_phys * c, c), c)
        pair_lanes = slice(pair * lanes, (pair + 1) * lanes)
        qf = q_ref[rows, pair_lanes].astype(F32) * (C_DK ** -0.5)
        kf = k_ref[rows, pair_lanes].astype(F32)
        z = (jnp.dot(lr_ref[rows, :], wd_ref[d, :, pair_lanes], preferred_element_type=F32)
             + bd_ref[d, :, pair_lanes])
        g = (jnp.minimum(z, 0.0) - jnp.log(1.0 + jnp.exp(-jnp.abs(z)))) * (1.0 / C_TAU)
        g_hi = g.astype(BF16)
        g_lo = (g - g_hi.astype(F32)).astype(BF16)
        e_ref[pair, d] = (jnp.dot(m_ref[d], g_hi, preferred_element_type=F32)
                          + jnp.dot(m_ref[d], g_lo, preferred_element_type=F32))
        yield

        q_head = [jnp.where(head_of_lane == h, qf, 0.0) for h in range(2)]
        a = jnp.where(xr == 0, _dot_nt(stack_heads(q_head), kf.astype(BF16)), 0.0)
        yield
        for l in range(nl):
            t = jnp.exp(e_ref[pair, d, l * c:(l + 1) * c, :])
            is_query_row = ((row >> l) & 1) == (1 - d)
            k_l = jnp.where(is_query_row, 0.0, kf * t).astype(BF16)
            q_l = stack_heads([jnp.where(is_query_row, q * t, 0.0) for q in q_head])
            p = _dot_nt(q_l, k_l)
            a = a + (jnp.where(xr < (2 << l), p, 0.0) if l < nl - 1 else p)
            yield

        t_in = jnp.exp(e_ref[pair, d, nl * c:(nl + 1) * c, :])
        t_out = jnp.exp(e_ref[pair, d, (nl + 1) * c:(nl + 2) * c, :])
        k_out = (kf * t_out).astype(BF16)
        q_in = stack_heads([q * t_in for q in q_head])
        a_bf = a.astype(BF16)
        last = c - 1 if d == 0 else 0
        decay_all = t_in[last:last + 1, :]
        yield
        for h in range(2):
            head = 2 * pair + h
            cols = slice(head * HEAD_DIM, (head + 1) * HEAD_DIM)
            v_h = v_ref[rows, cols]
            st = st_ref[d, head]
            acc_ref[d, rows, cols] = (
                jnp.dot(a_bf[h * c:(h + 1) * c], v_h, preferred_element_type=F32)
                + _dot_nt(q_in[h * c:(h + 1) * c], st.astype(BF16)))
            st_ref[d, head] = st * decay_all + _dot_tn(v_h, k_out)
            yield

    def scan_step(ci, carry):
        scans = [scan_chunk(pair, d, ci) for pair in range(2) for d in range(2)]
        while scans:
            scans = [s for s in scans if next(s, "done") is None]
        return carry

    lax.fori_loop(0, nch, scan_step, 0)

    def finish(ci, carry):
        rows = pl.ds(pl.multiple_of(ci * c, c), c)
        for h in range(C_V // HEAD_DIM):
            cols = slice(h * HEAD_DIM, (h + 1) * HEAD_DIM)
            y = _rms(acc_ref[0, rows, cols] + acc_ref[1, rows, cols], onorm_ref[...])
            gate = og_ref[rows, cols].astype(F32)
            o_ref[rows, cols] = (y * gate * _sigmoid(gate)).astype(BF16)
        return carry

    lax.fori_loop(0, nch, finish, 0)


def _gla(proj, bsz, w_dec, b_dec, onorm, scan_m, layer):
    t = proj.shape[0]
    lanes = 2 * C_DK
    n_e = (GLA_LEVELS + 2) * GLA_C
    return pl.pallas_call(
        _gla_kernel,
        grid=(bsz,),
        in_specs=[
            pl.BlockSpec((SEQ, C_K), lambda b: (b, COL_CQ // C_K)),
            pl.BlockSpec((SEQ, C_K), lambda b: (b, COL_CK // C_K)),
            pl.BlockSpec((SEQ, C_V), lambda b: (b, COL_CV // C_V)),
            pl.BlockSpec((SEQ, C_V), lambda b: (b, COL_COG // C_V)),
            pl.BlockSpec((SEQ, lanes), lambda b: (b, COL_CL // lanes)),
            pl.BlockSpec((None, 2, lanes, C_K), lambda b: (layer, 0, 0, 0)),
            pl.BlockSpec((None, 2, 1, C_K), lambda b: (layer, 0, 0, 0)),
            pl.BlockSpec((None, 1, HEAD_DIM), lambda b: (layer, 0, 0)),
            pl.BlockSpec((2, n_e, GLA_C), lambda b: (0, 0, 0)),
        ],
        out_specs=pl.BlockSpec((SEQ, C_V), lambda b: (b, 0)),
        out_shape=jax.ShapeDtypeStruct((t, C_V), BF16),
        scratch_shapes=[
            pltpu.VMEM((2, SEQ, C_V), F32),
            pltpu.VMEM((2, 2, n_e, lanes), F32),
            pltpu.VMEM((2, C_V // HEAD_DIM, HEAD_DIM, lanes), F32),
        ],
        compiler_params=_params("parallel"),
        name="gla",
    )(proj, proj, proj, proj, proj, w_dec, b_dec, onorm, scan_m)


def _merge_kernel(x_ref, oa_ref, ob_ref, oc_ref, ga_ref, gb_ref, gc_ref, gbias_ref,
                  wa_ref, wb_ref, wc_ref, wo_ref, ng_ref, o_ref):
    j = pl.program_id(1)

    @pl.when(j == 0)
    def _():
        o_ref[...] = jnp.zeros_like(o_ref)

    def branch(o_b, w_b, g_b, row):
        y = jnp.dot(o_b[...], w_b[...], preferred_element_type=F32)
        return _sigmoid(g_b[...].astype(F32) + gbias_ref[row:row + 1, :]) * y

    merged = (branch(oa_ref, wa_ref, ga_ref, 0) + branch(ob_ref, wb_ref, gb_ref, 1)
              + branch(oc_ref, wc_ref, gc_ref, 2))
    o_ref[...] += jnp.dot(merged.astype(BF16), wo_ref[...], preferred_element_type=F32)

    @pl.when(j == pl.num_programs(1) - 1)
    def _():
        o_ref[...] = x_ref[...] + _rms(o_ref[...], ng_ref[3:4, :])


def _merge(x, oa, ob, oc, proj, gate_bias, wa, wb, wc, wo, ng, layer):
    t = x.shape[0]
    tm, tn = MERGE_TM, MERGE_TN
    return pl.pallas_call(
        _merge_kernel,
        grid=(t // tm, D_MODEL // tn),
        in_specs=[
            pl.BlockSpec((tm, D_MODEL), lambda i, j: (i, 0)),
            pl.BlockSpec((tm, A_Q), lambda i, j: (i, 0)),
            pl.BlockSpec((tm, B_W), lambda i, j: (i, 0)),
            pl.BlockSpec((tm, C_V), lambda i, j: (i, 0)),
            pl.BlockSpec((tm, tn), lambda i, j: (i, COL_GA // tn + j)),
            pl.BlockSpec((tm, tn), lambda i, j: (i, COL_GB // tn + j)),
            pl.BlockSpec((tm, tn), lambda i, j: (i, COL_GC // tn + j)),
            pl.BlockSpec((None, 3, tn), lambda i, j: (layer, 0, j)),
            pl.BlockSpec((None, A_Q, tn), lambda i, j: (layer, 0, j)),
            pl.BlockSpec((None, B_W, tn), lambda i, j: (layer, 0, j)),
            pl.BlockSpec((None, C_V, tn), lambda i, j: (layer, 0, j)),
            pl.BlockSpec((None, tn, D_MODEL), lambda i, j: (layer, j, 0)),
            pl.BlockSpec((None, 6, D_MODEL), lambda i, j: (layer, 0, 0)),
        ],
        out_specs=pl.BlockSpec((tm, D_MODEL), lambda i, j: (i, 0)),
        out_shape=jax.ShapeDtypeStruct((t, D_MODEL), F32),
        compiler_params=_params("parallel", "arbitrary", vmem_limit=MERGE_VMEM_LIMIT),
        name="merge",
    )(x, oa, ob, oc, proj, proj, proj, gate_bias, wa, wb, wc, wo, ng)


def _rope_tables():
    t = jnp.arange(SEQ)
    half = HEAD_DIM // 2
    inv = ROPE_THETA ** (-jnp.arange(0, half, 2, dtype=F32) / half)
    pos_r = (t // GRID_W).astype(F32)
    pos_c = (t % GRID_W).astype(F32)
    ang = jnp.concatenate([pos_r[:, None] * inv, pos_c[:, None] * inv], axis=-1)
    cos = jnp.repeat(jnp.cos(ang), 2, axis=-1)
    sign = jnp.asarray(np.tile(np.array([-1.0, 1.0], np.float32), half))
    sin_signed = jnp.repeat(jnp.sin(ang), 2, axis=-1) * sign
    return cos.astype(F32), sin_signed.astype(F32)


def _relayout_w_in(w_in):
    n_head = COL_CL + 2 * C_RANK
    pad = jnp.zeros(w_in.shape[:2] + (COL_GA - n_head,), BF16)
    return jnp.concatenate([w_in[..., :n_head].astype(BF16), pad, w_in[..., n_head:].astype(BF16)],
                           axis=-1)


def _tile_columns(w, tn):
    n_layers, k, n = w.shape
    return w.reshape(n_layers, k, n // tn, tn).transpose(0, 2, 1, 3)


def _decay_weights(w_decay, b_decay):
    lanes = 2 * C_DK
    zeros = jnp.zeros((w_decay.shape[0], lanes - 2 * C_RANK, C_K), w_decay.dtype)
    zero_r = jnp.zeros((w_decay.shape[0], C_RANK, C_K), w_decay.dtype)
    w_f = jnp.concatenate([w_decay[:, 0], zero_r, zeros], axis=1)
    w_b = jnp.concatenate([zero_r, w_decay[:, 1], zeros], axis=1)
    return jnp.stack([w_f, w_b], axis=1).astype(BF16), b_decay[:, :, None, :].astype(F32)


def kernel(x_prompt, x_sample, norm_gains, w_in, gate_bias, qk_norm_a, rpb_b, w_decay_c, b_decay_c,
           onorm_c, w_br_a, w_br_b, w_br_c, w_out, w_ffn1_in, w_ffn1_out, w_ffn2_in, w_ffn2_out):
    assert x_prompt.shape[1:] == (SEQ, D_MODEL) and x_sample.shape[1:] == (SEQ, D_MODEL)
    cos, sin_signed = _rope_tables()
    w_proj = _relayout_w_in(w_in)
    w_dec, b_dec = _decay_weights(w_decay_c, b_decay_c)
    nb_bias = _nb_bias_tables(rpb_b)
    scan_m = _gla_scan_matrices()
    wa, wb, wc, wo = (w.astype(BF16) for w in (w_br_a, w_br_b, w_br_c, w_out))
    f1i, f2i = (_tile_columns(w.astype(BF16), FFN_TF) for w in (w_ffn1_in, w_ffn2_in))
    f1o, f2o = (w.astype(BF16) for w in (w_ffn1_out, w_ffn2_out))
    onorm = onorm_c[:, None, :]

    def trunk(x):
        bsz = x.shape[0]
        x = x.reshape(bsz * SEQ, D_MODEL)
        for l in range(DEPTH):
            x = _ffn(x, norm_gains, f1i, f1o, l, 0, 1)
            proj = _proj(x, norm_gains, w_proj, l)
            oa = _attn_a(proj, bsz, cos, sin_signed, qk_norm_a, l)
            ob = _attn_b(proj, bsz, nb_bias, l)
            oc = _gla(proj, bsz, w_dec, b_dec, onorm, scan_m, l)
            x = _merge(x, oa, ob, oc, proj, gate_bias, wa, wb, wc, wo, norm_gains, l)
            x = _ffn(x, norm_gains, f2i, f2o, l, 4, 5)
        return x.reshape(bsz, SEQ, D_MODEL)

    return trunk(x_prompt), trunk(x_sample)
```

```python
import functools
import math

import jax
import jax.numpy as jnp
import numpy as np
from jax import lax
from jax.experimental import pallas as pl
from jax.experimental.pallas import tpu as pltpu

F32 = jnp.float32
BF16 = jnp.bfloat16

D_MODEL = 2048
SEQ = 2048
DEPTH = 4
GRID_W = 64
HEAD_DIM = 128
A_Q = 1024
A_KV = 256
A_GROUP = 4
A_KV_HEADS = 2
ROPE_THETA = 10000.0
B_W = 512
B_HEADS = 4
NA_ROWS = 8
NA_COLS = 16
C_V = 512
C_K = 256
C_DK = 64
C_RANK = 16
C_TAU = 16.0
D_FF = 5632
EPS = 1e-6

COL_AQ, COL_AK, COL_AV = 0, 1024, 1280
COL_BQ, COL_BK, COL_BV = 1536, 2048, 2560
COL_CQ, COL_CK, COL_CV, COL_COG = 3072, 3328, 3584, 4096
COL_CL = 4608
COL_GA, COL_GB, COL_GC = 5120, 7168, 9216
N_PROJ = 11264

NEG = -1e30
LOG2E = math.log2(math.e)
VMEM_LIMIT = 56 * 1024 * 1024

FFN_TM, FFN_TF = 1024, 256
PROJ_TM, PROJ_TN = 1024, 1024
MERGE_TM, MERGE_TN = 512, 512
FFN_VMEM_LIMIT = 60 * 1024 * 1024
ATT_TQ = 256
NB_QROWS = 8
NB_KROWS = 16
NB_TQ = NB_QROWS * GRID_W
NB_TK = NB_KROWS * GRID_W
GLA_C = 128
GLA_LEVELS = GLA_C.bit_length() - 1


def _rms(x, gain):
    ms = jnp.mean(x * x, axis=-1, keepdims=True)
    return x * lax.rsqrt(ms + EPS) * gain


def _sigmoid(x):
    return 1.0 / (1.0 + jnp.exp(-x))


def _dot_nt(a, b):
    return lax.dot_general(a, b, (((1,), (1,)), ((), ())), preferred_element_type=F32)


def _dot_tn(a, b):
    return lax.dot_general(a, b, (((0,), (0,)), ((), ())), preferred_element_type=F32)


def _params(*sem, vmem_limit=VMEM_LIMIT):
    return pltpu.CompilerParams(dimension_semantics=sem, vmem_limit_bytes=vmem_limit)


def _ffn_kernel(x_ref, ng_ref, wg_ref, wu_ref, wo_ref, o_ref, xn_ref, *, row_in, row_out):
    f = pl.program_id(1)

    @pl.when(f == 0)
    def _():
        xn_ref[...] = _rms(x_ref[...], ng_ref[row_in:row_in + 1, :]).astype(BF16)
        o_ref[...] = jnp.zeros_like(o_ref)

    xn = xn_ref[...]
    g = jnp.dot(xn, wg_ref[...], preferred_element_type=F32)
    u = jnp.dot(xn, wu_ref[...], preferred_element_type=F32)
    a = (g * _sigmoid(g) * u).astype(BF16)
    o_ref[...] += jnp.dot(a, wo_ref[...], preferred_element_type=F32)

    @pl.when(f == pl.num_programs(1) - 1)
    def _():
        y = _rms(o_ref[...], ng_ref[row_out:row_out + 1, :])
        o_ref[...] = x_ref[...] + 0.5 * y


def _ffn(x, ng, w_in, w_out, layer, row_in, row_out):
    t = x.shape[0]
    nf = D_FF // FFN_TF
    return pl.pallas_call(
        functools.partial(_ffn_kernel, row_in=row_in, row_out=row_out),
        grid=(t // FFN_TM, nf),
        in_specs=[
            pl.BlockSpec((FFN_TM, D_MODEL), lambda i, f: (i, 0)),
            pl.BlockSpec((None, 6, D_MODEL), lambda i, f: (layer, 0, 0)),
            pl.BlockSpec((None, D_MODEL, FFN_TF), lambda i, f: (layer, 0, f)),
            pl.BlockSpec((None, D_MODEL, FFN_TF), lambda i, f: (layer, 0, f + nf)),
            pl.BlockSpec((None, FFN_TF, D_MODEL), lambda i, f: (layer, f, 0)),
        ],
        out_specs=pl.BlockSpec((FFN_TM, D_MODEL), lambda i, f: (i, 0)),
        out_shape=jax.ShapeDtypeStruct((t, D_MODEL), F32),
        scratch_shapes=[pltpu.VMEM((FFN_TM, D_MODEL), BF16)],
        compiler_params=_params("parallel", "arbitrary", vmem_limit=FFN_VMEM_LIMIT),
        name="ffn",
    )(x, ng, w_in, w_in, w_out)


def _proj_kernel(x_ref, ng_ref, w_ref, o_ref, xn_ref):
    @pl.when(pl.program_id(1) == 0)
    def _():
        xn_ref[...] = _rms(x_ref[...], ng_ref[2:3, :]).astype(BF16)

    o_ref[...] = jnp.dot(xn_ref[...], w_ref[...], preferred_element_type=F32).astype(BF16)


def _proj(x, ng, w, layer):
    t = x.shape[0]
    return pl.pallas_call(
        _proj_kernel,
        grid=(t // PROJ_TM, N_PROJ // PROJ_TN),
        in_specs=[
            pl.BlockSpec((PROJ_TM, D_MODEL), lambda i, j: (i, 0)),
            pl.BlockSpec((None, 6, D_MODEL), lambda i, j: (layer, 0, 0)),
            pl.BlockSpec((None, D_MODEL, PROJ_TN), lambda i, j: (layer, 0, j)),
        ],
        out_specs=pl.BlockSpec((PROJ_TM, PROJ_TN), lambda i, j: (i, j)),
        out_shape=jax.ShapeDtypeStruct((t, N_PROJ), BF16),
        scratch_shapes=[pltpu.VMEM((PROJ_TM, D_MODEL), BF16)],
        compiler_params=_params("parallel", "arbitrary"),
        name="proj",
    )(x, ng, w)


def _norm_rope(x, gain, cos, sin_signed, even_lane):
    y = _rms(x, gain)
    swapped = jnp.where(even_lane, pltpu.roll(y, HEAD_DIM - 1, 1), pltpu.roll(y, 1, 1))
    return y * cos + swapped * sin_signed


def _with_ones(v_ref, v1_ref):
    v1_ref[:, :HEAD_DIM] = v_ref[...]
    v1_ref[:, HEAD_DIM:] = jnp.ones((v_ref.shape[0], HEAD_DIM), BF16)


def _softmax_pv(s, v1):
    m = jnp.max(s, axis=-1, keepdims=True)
    p = jnp.exp2(s - m).astype(BF16)
    ov = jnp.dot(p, v1, preferred_element_type=F32)
    return ov[:, :HEAD_DIM] / ov[:, HEAD_DIM:]


def _attn_a_kernel(q_ref, k_ref, v_ref, cq_ref, sq_ref, ck_ref, sk_ref, gain_ref, o_ref,
                   kn_ref, v1_ref):
    @pl.when(pl.program_id(2) == 0)
    def _():
        even_k = (lax.broadcasted_iota(jnp.int32, (SEQ, HEAD_DIM), 1) & 1) == 0
        k = k_ref[...].astype(F32)
        kn_ref[...] = _norm_rope(k, gain_ref[1:2, :], ck_ref[...], sk_ref[...], even_k).astype(BF16)
        _with_ones(v_ref, v1_ref)

    even_q = (lax.broadcasted_iota(jnp.int32, (ATT_TQ, HEAD_DIM), 1) & 1) == 0
    kn = kn_ref[...]
    v = v1_ref[...]
    cq = cq_ref[...]
    sq = sq_ref[...]
    scale = HEAD_DIM ** -0.5 * LOG2E

    def scores(h):
        q = q_ref[:, h * HEAD_DIM:(h + 1) * HEAD_DIM].astype(F32)
        qn = (_norm_rope(q, gain_ref[0:1, :], cq, sq, even_q) * scale).astype(BF16)
        return _dot_nt(qn, kn)

    s_next = scores(0)
    for h in range(A_GROUP):
        s = s_next
        if h + 1 < A_GROUP:
            s_next = scores(h + 1)
        o_ref[:, h * HEAD_DIM:(h + 1) * HEAD_DIM] = _softmax_pv(s, v).astype(BF16)


def _attn_a(proj, bsz, cos, sin_signed, qk_gain, layer):
    t = proj.shape[0]
    nq = SEQ // ATT_TQ
    qw = A_GROUP * HEAD_DIM
    return pl.pallas_call(
        _attn_a_kernel,
        grid=(bsz, A_KV_HEADS, nq),
        in_specs=[
            pl.BlockSpec((ATT_TQ, qw), lambda b, g, i: (b * nq + i, COL_AQ // qw + g)),
            pl.BlockSpec((SEQ, HEAD_DIM), lambda b, g, i: (b, COL_AK // HEAD_DIM + g)),
            pl.BlockSpec((SEQ, HEAD_DIM), lambda b, g, i: (b, COL_AV // HEAD_DIM + g)),
            pl.BlockSpec((ATT_TQ, HEAD_DIM), lambda b, g, i: (i, 0)),
            pl.BlockSpec((ATT_TQ, HEAD_DIM), lambda b, g, i: (i, 0)),
            pl.BlockSpec((SEQ, HEAD_DIM), lambda b, g, i: (0, 0)),
            pl.BlockSpec((SEQ, HEAD_DIM), lambda b, g, i: (0, 0)),
            pl.BlockSpec((None, 2, HEAD_DIM), lambda b, g, i: (layer, 0, 0)),
        ],
        out_specs=pl.BlockSpec((ATT_TQ, qw), lambda b, g, i: (b * nq + i, g)),
        out_shape=jax.ShapeDtypeStruct((t, A_Q), BF16),
        scratch_shapes=[pltpu.VMEM((SEQ, HEAD_DIM), BF16), pltpu.VMEM((SEQ, 2 * HEAD_DIM), BF16)],
        compiler_params=_params("parallel", "arbitrary", "arbitrary"),
        name="attn_a",
    )(proj, proj, proj, cos, sin_signed, cos, sin_signed, qk_gain)


def _nb_key_row_start(r):
    rows = SEQ // GRID_W
    return int(np.clip(r * NB_QROWS - NA_ROWS // 2, 0, rows - NB_KROWS))


def _attn_b_kernel(q_ref, k_ref, v_ref, bias_ref, o_ref, v1_ref):
    scale = HEAD_DIM ** -0.5 * LOG2E
    nr = SEQ // NB_TQ
    key_start = [_nb_key_row_start(r) * GRID_W for r in range(nr)]
    _with_ones(v_ref, v1_ref)

    def scores(r):
        q = (q_ref[r * NB_TQ:(r + 1) * NB_TQ, :].astype(F32) * scale).astype(BF16)
        return _dot_nt(q, k_ref[key_start[r]:key_start[r] + NB_TK, :]) + bias_ref[0, r]

    s_next = scores(0)
    for r in range(nr):
        s = s_next
        if r + 1 < nr:
            s_next = scores(r + 1)
        o = _softmax_pv(s, v1_ref[key_start[r]:key_start[r] + NB_TK, :])
        o_ref[r * NB_TQ:(r + 1) * NB_TQ, :] = o.astype(BF16)


def _attn_b(proj, bsz, bias, layer):
    t = proj.shape[0]
    nr = SEQ // NB_TQ
    return pl.pallas_call(
        _attn_b_kernel,
        grid=(B_HEADS, bsz),
        in_specs=[
            pl.BlockSpec((SEQ, HEAD_DIM), lambda h, b: (b, COL_BQ // HEAD_DIM + h)),
            pl.BlockSpec((SEQ, HEAD_DIM), lambda h, b: (b, COL_BK // HEAD_DIM + h)),
            pl.BlockSpec((SEQ, HEAD_DIM), lambda h, b: (b, COL_BV // HEAD_DIM + h)),
            pl.BlockSpec((None, 1, nr, NB_TQ, NB_TK), lambda h, b: (layer, h, 0, 0, 0)),
        ],
        out_specs=pl.BlockSpec((SEQ, HEAD_DIM), lambda h, b: (b, h)),
        out_shape=jax.ShapeDtypeStruct((t, B_W), BF16),
        scratch_shapes=[pltpu.VMEM((SEQ, 2 * HEAD_DIM), BF16)],
        compiler_params=_params("arbitrary", "arbitrary"),
        name="attn_b",
    )(proj, proj, proj, bias)


def _nb_bias_tables(rpb):
    rows = SEQ // GRID_W
    nr = SEQ // NB_TQ
    n_dr, n_dc = 2 * NA_ROWS - 1, 2 * NA_COLS - 1
    row_sel = np.zeros((nr, NB_QROWS, NB_KROWS, n_dr), np.float32)
    for r in range(nr):
        for rq in range(NB_QROWS):
            gq = r * NB_QROWS + rq
            rs = int(np.clip(gq - NA_ROWS // 2, 0, rows - NA_ROWS))
            for rk in range(NB_KROWS):
                gk = _nb_key_row_start(r) + rk
                if rs <= gk < rs + NA_ROWS:
                    row_sel[r, rq, rk, gk - gq + NA_ROWS - 1] = 1.0
    col_sel = np.zeros((GRID_W, GRID_W, n_dc), np.float32)
    for cq in range(GRID_W):
        cs = int(np.clip(cq - NA_COLS // 2, 0, GRID_W - NA_COLS))
        for ck in range(cs, cs + NA_COLS):
            col_sel[cq, ck, ck - cq + NA_COLS - 1] = 1.0
    valid = np.einsum("rqkd,xye->rqxky", row_sel, col_sel) > 0.5
    t1 = jnp.einsum("rqkd,lhde->lhrqke", jnp.asarray(row_sel), rpb, precision=lax.Precision.HIGHEST)
    tab = jnp.einsum("lhrqke,xye->lhrqxky", t1, jnp.asarray(col_sel), precision=lax.Precision.HIGHEST)
    tab = jnp.where(jnp.asarray(valid)[None, None], tab * LOG2E, NEG)
    return tab.reshape(rpb.shape[0], B_HEADS, nr, NB_TQ, NB_TK)


def _gla_scan_matrices():
    c, nl = GLA_C, GLA_LEVELS
    m = np.zeros((2, nl + 2, c, c), np.float32)
    for d in range(2):
        tau = np.arange(c) if d == 0 else c - 1 - np.arange(c)
        tx, ty = tau[:, None], tau[None, :]
        for l in range(nl):
            left_end = ((tx >> (l + 1)) << (l + 1)) + (1 << l) - 1
            right = ((tx >> l) & 1) == 1
            m[d, l] = np.where(right, (ty > left_end) & (ty <= tx), (ty > tx) & (ty <= left_end))
        m[d, nl] = ty <= tx
        m[d, nl + 1] = ty > tx
    return jnp.asarray(m.reshape(2, (nl + 2) * c, c), BF16)


def _gla_kernel(q_ref, k_ref, v_ref, og_ref, lr_ref, wd_ref, bd_ref, onorm_ref, m_ref, o_ref,
                acc_ref, e_ref, st_ref):
    c, nl = GLA_C, GLA_LEVELS
    nch = SEQ // c
    lanes = 2 * C_DK

    row = lax.broadcasted_iota(jnp.int32, (c, lanes), 0)
    head_of_lane = lax.broadcasted_iota(jnp.int32, (c, lanes), 1) >> 6
    xr = ((lax.broadcasted_iota(jnp.int32, (2 * c, c), 0) & (c - 1))
          ^ lax.broadcasted_iota(jnp.int32, (2 * c, c), 1))

    st_ref[...] = jnp.zeros_like(st_ref)

    def stack_heads(per_head):
        return jnp.concatenate([x.astype(BF16) for x in per_head], axis=0)

    def scan_chunk(pair, d, ci):
        c_phys = ci if d == 0 else nch - 1 - ci
        rows = pl.ds(pl.multiple_of(c_phys * c, c), c)
        pair_lanes = slice(pair * lanes, (pair + 1) * lanes)
        qf = q_ref[rows, pair_lanes].astype(F32) * (C_DK ** -0.5)
        kf = k_ref[rows, pair_lanes].astype(F32)
        z = (jnp.dot(lr_ref[rows, :], wd_ref[d, :, pair_lanes], preferred_element_type=F32)
             + bd_ref[d, :, pair_lanes])
        g = (jnp.minimum(z, 0.0) - jnp.log(1.0 + jnp.exp(-jnp.abs(z)))) * (1.0 / C_TAU)
        g_hi = g.astype(BF16)
        g_lo = (g - g_hi.astype(F32)).astype(BF16)
        e_ref[pair, d] = (jnp.dot(m_ref[d], g_hi, preferred_element_type=F32)
                          + jnp.dot(m_ref[d], g_lo, preferred_element_type=F32))
        yield

        q_head = [jnp.where(head_of_lane == h, qf, 0.0) for h in range(2)]
        a = jnp.where(xr == 0, _dot_nt(stack_heads(q_head), kf.astype(BF16)), 0.0)
        yield
        for l in range(nl):
            t = jnp.exp(e_ref[pair, d, l * c:(l + 1) * c, :])
            is_query_row = ((row >> l) & 1) == (1 - d)
            k_l = jnp.where(is_query_row, 0.0, kf * t).astype(BF16)
            q_l = stack_heads([jnp.where(is_query_row, q * t, 0.0) for q in q_head])
            p = _dot_nt(q_l, k_l)
            a = a + (jnp.where(xr < (2 << l), p, 0.0) if l < nl - 1 else p)
            yield

        t_in = jnp.exp(e_ref[pair, d, nl * c:(nl + 1) * c, :])
        t_out = jnp.exp(e_ref[pair, d, (nl + 1) * c:(nl + 2) * c, :])
        k_out = (kf * t_out).astype(BF16)
        q_in = stack_heads([q * t_in for q in q_head])
        a_bf = a.astype(BF16)
        last = c - 1 if d == 0 else 0
        decay_all = t_in[last:last + 1, :]
        yield
        for h in range(2):
            head = 2 * pair + h
            cols = slice(head * HEAD_DIM, (head + 1) * HEAD_DIM)
            v_h = v_ref[rows, cols]
            st = st_ref[d, head]
            acc_ref[d, rows, cols] = (
                jnp.dot(a_bf[h * c:(h + 1) * c], v_h, preferred_element_type=F32)
                + _dot_nt(q_in[h * c:(h + 1) * c], st.astype(BF16)))
            st_ref[d, head] = st * decay_all + _dot_tn(v_h, k_out)
            yield

    def scan_step(ci, carry):
        scans = [scan_chunk(pair, d, ci) for pair in range(2) for d in range(2)]
        while scans:
            scans = [s for s in scans if next(s, "done") is None]
        return carry

    lax.fori_loop(0, nch, scan_step, 0)

    def finish(ci, carry):
        rows = pl.ds(pl.multiple_of(ci * c, c), c)
        for h in range(C_V // HEAD_DIM):
            cols = slice(h * HEAD_DIM, (h + 1) * HEAD_DIM)
            y = _rms(acc_ref[0, rows, cols] + acc_ref[1, rows, cols], onorm_ref[...])
            gate = og_ref[rows, cols].astype(F32)
            o_ref[rows, cols] = (y * gate * _sigmoid(gate)).astype(BF16)
        return carry

    lax.fori_loop(0, nch, finish, 0)


def _gla(proj, bsz, w_dec, b_dec, onorm, scan_m, layer):
    t = proj.shape[0]
    lanes = 2 * C_DK
    n_e = (GLA_LEVELS + 2) * GLA_C
    return pl.pallas_call(
        _gla_kernel,
        grid=(bsz,),
        in_specs=[
            pl.BlockSpec((SEQ, C_K), lambda b: (b, COL_CQ // C_K)),
            pl.BlockSpec((SEQ, C_K), lambda b: (b, COL_CK // C_K)),
            pl.BlockSpec((SEQ, C_V), lambda b: (b, COL_CV // C_V)),
            pl.BlockSpec((SEQ, C_V), lambda b: (b, COL_COG // C_V)),
            pl.BlockSpec((SEQ, lanes), lambda b: (b, COL_CL // lanes)),
            pl.BlockSpec((None, 2, lanes, C_K), lambda b: (layer, 0, 0, 0)),
            pl.BlockSpec((None, 2, 1, C_K), lambda b: (layer, 0, 0, 0)),
            pl.BlockSpec((None, 1, HEAD_DIM), lambda b: (layer, 0, 0)),
            pl.BlockSpec((2, n_e, GLA_C), lambda b: (0, 0, 0)),
        ],
        out_specs=pl.BlockSpec((SEQ, C_V), lambda b: (b, 0)),
        out_shape=jax.ShapeDtypeStruct((t, C_V), BF16),
        scratch_shapes=[
            pltpu.VMEM((2, SEQ, C_V), F32),
            pltpu.VMEM((2, 2, n_e, lanes), F32),
            pltpu.VMEM((2, C_V // HEAD_DIM, HEAD_DIM, lanes), F32),
        ],
        compiler_params=_params("parallel"),
        name="gla",
    )(proj, proj, proj, proj, proj, w_dec, b_dec, onorm, scan_m)


def _merge_kernel(x_ref, oa_ref, ob_ref, oc_ref, ga_ref, gb_ref, gc_ref, gbias_ref,
                  wa_ref, wb_ref, wc_ref, wo_ref, ng_ref, o_ref, m_ref):
    j = pl.program_id(1)
    tn = MERGE_TN
    n_early = D_MODEL // tn - 1

    def branch_products():
        return [jnp.dot(o_b[...], w_b[...], preferred_element_type=F32)
                for o_b, w_b in ((oa_ref, wa_ref), (ob_ref, wb_ref), (oc_ref, wc_ref))]

    def gated_sum(ys):
        gates = (ga_ref, gb_ref, gc_ref)
        return sum(_sigmoid(gates[b][...].astype(F32) + gbias_ref[b:b + 1, :]) * ys[b]
                   for b in range(3)).astype(BF16)

    @pl.when(j < n_early)
    def _():
        m_ref[j] = gated_sum(branch_products())

    @pl.when(j == n_early)
    def _():
        ys = branch_products()
        acc = jnp.dot(m_ref[0], wo_ref[0:tn, :], preferred_element_type=F32)
        for t in range(1, n_early):
            acc += jnp.dot(m_ref[t], wo_ref[t * tn:(t + 1) * tn, :], preferred_element_type=F32)
        acc += jnp.dot(gated_sum(ys), wo_ref[n_early * tn:, :], preferred_element_type=F32)
        o_ref[...] = x_ref[...] + _rms(acc, ng_ref[3:4, :])


def _merge(x, oa, ob, oc, proj, gate_bias, wa, wb, wc, wo, ng, layer):
    t = x.shape[0]
    tm, tn = MERGE_TM, MERGE_TN
    return pl.pallas_call(
        _merge_kernel,
        grid=(t // tm, D_MODEL // tn),
        in_specs=[
            pl.BlockSpec((tm, D_MODEL), lambda i, j: (i, 0)),
            pl.BlockSpec((tm, A_Q), lambda i, j: (i, 0)),
            pl.BlockSpec((tm, B_W), lambda i, j: (i, 0)),
            pl.BlockSpec((tm, C_V), lambda i, j: (i, 0)),
            pl.BlockSpec((tm, tn), lambda i, j: (i, COL_GA // tn + j)),
            pl.BlockSpec((tm, tn), lambda i, j: (i, COL_GB // tn + j)),
            pl.BlockSpec((tm, tn), lambda i, j: (i, COL_GC // tn + j)),
            pl.BlockSpec((None, 3, tn), lambda i, j: (layer, 0, j)),
            pl.BlockSpec((None, A_Q, tn), lambda i, j: (layer, 0, j)),
            pl.BlockSpec((None, B_W, tn), lambda i, j: (layer, 0, j)),
            pl.BlockSpec((None, C_V, tn), lambda i, j: (layer, 0, j)),
            pl.BlockSpec((None, D_MODEL, D_MODEL), lambda i, j: (layer, 0, 0)),
            pl.BlockSpec((None, 6, D_MODEL), lambda i, j: (layer, 0, 0)),
        ],
        out_specs=pl.BlockSpec((tm, D_MODEL), lambda i, j: (i, 0)),
        out_shape=jax.ShapeDtypeStruct((t, D_MODEL), F32),
        scratch_shapes=[pltpu.VMEM((D_MODEL // tn - 1, tm, tn), BF16)],
        compiler_params=_params("parallel", "arbitrary"),
        name="merge",
    )(x, oa, ob, oc, proj, proj, proj, gate_bias, wa, wb, wc, wo, ng)


def _rope_tables():
    t = jnp.arange(SEQ)
    half = HEAD_DIM // 2
    inv = ROPE_THETA ** (-jnp.arange(0, half, 2, dtype=F32) / half)
    pos_r = (t // GRID_W).astype(F32)
    pos_c = (t % GRID_W).astype(F32)
    ang = jnp.concatenate([pos_r[:, None] * inv, pos_c[:, None] * inv], axis=-1)
    cos = jnp.repeat(jnp.cos(ang), 2, axis=-1)
    sign = jnp.asarray(np.tile(np.array([-1.0, 1.0], np.float32), half))
    sin_signed = jnp.repeat(jnp.sin(ang), 2, axis=-1) * sign
    return cos.astype(F32), sin_signed.astype(F32)


def _relayout_w_in(w_in):
    n_head = COL_CL + 2 * C_RANK
    w = w_in.astype(BF16)
    pad = jnp.zeros(w.shape[:2] + (COL_GA - n_head,), BF16)
    return jnp.concatenate([w[..., :n_head], pad, w[..., n_head:]], axis=-1)


def _decay_weights(w_decay, b_decay):
    lanes = 2 * C_DK
    zeros = jnp.zeros((w_decay.shape[0], lanes - 2 * C_RANK, C_K), w_decay.dtype)
    zero_r = jnp.zeros((w_decay.shape[0], C_RANK, C_K), w_decay.dtype)
    w_f = jnp.concatenate([w_decay[:, 0], zero_r, zeros], axis=1)
    w_b = jnp.concatenate([zero_r, w_decay[:, 1], zeros], axis=1)
    return jnp.stack([w_f, w_b], axis=1).astype(BF16), b_decay[:, :, None, :].astype(F32)


def kernel(x_prompt, x_sample, norm_gains, w_in, gate_bias, qk_norm_a, rpb_b, w_decay_c, b_decay_c,
           onorm_c, w_br_a, w_br_b, w_br_c, w_out, w_ffn1_in, w_ffn1_out, w_ffn2_in, w_ffn2_out):
    assert x_prompt.shape[1:] == (SEQ, D_MODEL) and x_sample.shape[1:] == (SEQ, D_MODEL)
    cos, sin_signed = _rope_tables()
    w_proj = _relayout_w_in(w_in)
    w_dec, b_dec = _decay_weights(w_decay_c, b_decay_c)
    nb_bias = _nb_bias_tables(rpb_b)
    scan_m = _gla_scan_matrices()
    wa, wb, wc, wo = (w.astype(BF16) for w in (w_br_a, w_br_b, w_br_c, w_out))
    f1i, f1o, f2i, f2o = (w.astype(BF16) for w in (w_ffn1_in, w_ffn1_out, w_ffn2_in, w_ffn2_out))
    onorm = onorm_c[:, None, :]

    def trunk(x):
        bsz = x.shape[0]
        x = x.reshape(bsz * SEQ, D_MODEL)
        for l in range(DEPTH):
            x = _ffn(x, norm_gains, f1i, f1o, l, 0, 1)
            proj = _proj(x, norm_gains, w_proj, l)
            oa = _attn_a(proj, bsz, cos, sin_signed, qk_norm_a, l)
            ob = _attn_b(proj, bsz, nb_bias, l)
            oc = _gla(proj, bsz, w_dec, b_dec, onorm, scan_m, l)
            x = _merge(x, oa, ob, oc, proj, gate_bias, wa, wb, wc, wo, norm_gains, l)
            x = _ffn(x, norm_gains, f2i, f2o, l, 4, 5)
        return x.reshape(bsz, SEQ, D_MODEL)

    return trunk(x_prompt), trunk(x_sample)
```

```python
import functools
import math

import jax
import jax.numpy as jnp
import numpy as np
from jax import lax
from jax.experimental import pallas as pl
from jax.experimental.pallas import tpu as pltpu

F32 = jnp.float32
BF16 = jnp.bfloat16

D_MODEL = 2048
SEQ = 2048
DEPTH = 4
GRID_W = 64
HEAD_DIM = 128
A_Q = 1024
A_KV = 256
A_GROUP = 4
A_KV_HEADS = 2
ROPE_THETA = 10000.0
B_W = 512
B_HEADS = 4
NA_ROWS = 8
NA_COLS = 16
C_V = 512
C_K = 256
C_DK = 64
C_RANK = 16
C_TAU = 16.0
D_FF = 5632
EPS = 1e-6

COL_AQ, COL_AK, COL_AV = 0, 1024, 1280
COL_BQ, COL_BK, COL_BV = 1536, 2048, 2560
COL_CQ, COL_CK, COL_CV, COL_COG = 3072, 3328, 3584, 4096
COL_CL = 4608
COL_GA, COL_GB, COL_GC = 5120, 7168, 9216
N_PROJ = 11264

NEG = -1e30
LOG2E = math.log2(math.e)
VMEM_LIMIT = 56 * 1024 * 1024

FFN_TM, FFN_TF = 1024, 256
PROJ_TM, PROJ_TN = 1024, 1024
MERGE_TM, MERGE_TN = 512, 512
FFN_VMEM_LIMIT = 60 * 1024 * 1024
ATT_TQ = 256
NB_QROWS = 8
NB_KROWS = 16
NB_TQ = NB_QROWS * GRID_W
NB_TK = NB_KROWS * GRID_W
GLA_C = 128
GLA_LEVELS = GLA_C.bit_length() - 1


def _rms(x, gain):
    ms = jnp.mean(x * x, axis=-1, keepdims=True)
    return x * lax.rsqrt(ms + EPS) * gain


def _sigmoid(x):
    return 1.0 / (1.0 + jnp.exp(-x))


def _dot_nt(a, b):
    return lax.dot_general(a, b, (((1,), (1,)), ((), ())), preferred_element_type=F32)


def _dot_tn(a, b):
    return lax.dot_general(a, b, (((0,), (0,)), ((), ())), preferred_element_type=F32)


def _params(*sem, vmem_limit=VMEM_LIMIT):
    return pltpu.CompilerParams(dimension_semantics=sem, vmem_limit_bytes=vmem_limit)


def _ffn_kernel(x_ref, ng_ref, wg_ref, wu_ref, wo_ref, o_ref, xn_ref, *, row_in, row_out):
    f = pl.program_id(1)

    @pl.when(f == 0)
    def _():
        xn_ref[...] = _rms(x_ref[...], ng_ref[row_in:row_in + 1, :]).astype(BF16)
        o_ref[...] = jnp.zeros_like(o_ref)

    xn = xn_ref[...]
    g = jnp.dot(xn, wg_ref[...], preferred_element_type=F32)
    u = jnp.dot(xn, wu_ref[...], preferred_element_type=F32)
    a = (g * _sigmoid(g) * u).astype(BF16)
    o_ref[...] += jnp.dot(a, wo_ref[...], preferred_element_type=F32)

    @pl.when(f == pl.num_programs(1) - 1)
    def _():
        y = _rms(o_ref[...], ng_ref[row_out:row_out + 1, :])
        o_ref[...] = x_ref[...] + 0.5 * y


def _ffn(x, ng, w_in, w_out, layer, row_in, row_out):
    t = x.shape[0]
    nf = D_FF // FFN_TF
    return pl.pallas_call(
        functools.partial(_ffn_kernel, row_in=row_in, row_out=row_out),
        grid=(t // FFN_TM, nf),
        in_specs=[
            pl.BlockSpec((FFN_TM, D_MODEL), lambda i, f: (i, 0)),
            pl.BlockSpec((None, 6, D_MODEL), lambda i, f: (layer, 0, 0)),
            pl.BlockSpec((None, D_MODEL, FFN_TF), lambda i, f: (layer, 0, f)),
            pl.BlockSpec((None, D_MODEL, FFN_TF), lambda i, f: (layer, 0, f + nf)),
            pl.BlockSpec((None, FFN_TF, D_MODEL), lambda i, f: (layer, f, 0)),
        ],
        out_specs=pl.BlockSpec((FFN_TM, D_MODEL), lambda i, f: (i, 0)),
        out_shape=jax.ShapeDtypeStruct((t, D_MODEL), F32),
        scratch_shapes=[pltpu.VMEM((FFN_TM, D_MODEL), BF16)],
        compiler_params=_params("parallel", "arbitrary", vmem_limit=FFN_VMEM_LIMIT),
        name="ffn",
    )(x, ng, w_in, w_in, w_out)


def _proj_kernel(x_ref, ng_ref, w_ref, o_ref, xn_ref):
    @pl.when(pl.program_id(1) == 0)
    def _():
        xn_ref[...] = _rms(x_ref[...], ng_ref[2:3, :]).astype(BF16)

    o_ref[...] = jnp.dot(xn_ref[...], w_ref[...], preferred_element_type=F32).astype(BF16)


def _proj(x, ng, w, layer):
    t = x.shape[0]
    return pl.pallas_call(
        _proj_kernel,
        grid=(t // PROJ_TM, N_PROJ // PROJ_TN),
        in_specs=[
            pl.BlockSpec((PROJ_TM, D_MODEL), lambda i, j: (i, 0)),
            pl.BlockSpec((None, 6, D_MODEL), lambda i, j: (layer, 0, 0)),
            pl.BlockSpec((None, D_MODEL, PROJ_TN), lambda i, j: (layer, 0, j)),
        ],
        out_specs=pl.BlockSpec((PROJ_TM, PROJ_TN), lambda i, j: (i, j)),
        out_shape=jax.ShapeDtypeStruct((t, N_PROJ), BF16),
        scratch_shapes=[pltpu.VMEM((PROJ_TM, D_MODEL), BF16)],
        compiler_params=_params("parallel", "arbitrary"),
        name="proj",
    )(x, ng, w)


def _norm_rope(x, gain, cos, sin_signed, even_lane):
    y = _rms(x, gain)
    swapped = jnp.where(even_lane, pltpu.roll(y, HEAD_DIM - 1, 1), pltpu.roll(y, 1, 1))
    return y * cos + swapped * sin_signed


def _with_ones(v_ref, v1_ref):
    v1_ref[:, :HEAD_DIM] = v_ref[...]
    v1_ref[:, HEAD_DIM:] = jnp.ones((v_ref.shape[0], HEAD_DIM), BF16)


def _softmax_pv(s, v1):
    m = jnp.max(s, axis=-1, keepdims=True)
    p = jnp.exp2(s - m).astype(BF16)
    ov = jnp.dot(p, v1, preferred_element_type=F32)
    return ov[:, :HEAD_DIM] / ov[:, HEAD_DIM:]


def _attn_a_kernel(q_ref, k_ref, v_ref, cq_ref, sq_ref, ck_ref, sk_ref, gain_ref, o_ref,
                   kn_ref, v1_ref):
    @pl.when(pl.program_id(2) == 0)
    def _():
        even_k = (lax.broadcasted_iota(jnp.int32, (SEQ, HEAD_DIM), 1) & 1) == 0
        k = k_ref[...].astype(F32)
        kn_ref[...] = _norm_rope(k, gain_ref[1:2, :], ck_ref[...], sk_ref[...], even_k).astype(BF16)
        _with_ones(v_ref, v1_ref)

    even_q = (lax.broadcasted_iota(jnp.int32, (ATT_TQ, HEAD_DIM), 1) & 1) == 0
    kn = kn_ref[...]
    v = v1_ref[...]
    cq = cq_ref[...]
    sq = sq_ref[...]
    scale = HEAD_DIM ** -0.5 * LOG2E

    def scores(h):
        q = q_ref[:, h * HEAD_DIM:(h + 1) * HEAD_DIM].astype(F32)
        qn = (_norm_rope(q, gain_ref[0:1, :], cq, sq, even_q) * scale).astype(BF16)
        return _dot_nt(qn, kn)

    s_next = scores(0)
    for h in range(A_GROUP):
        s = s_next
        if h + 1 < A_GROUP:
            s_next = scores(h + 1)
        o_ref[:, h * HEAD_DIM:(h + 1) * HEAD_DIM] = _softmax_pv(s, v).astype(BF16)


def _attn_a(proj, bsz, cos, sin_signed, qk_gain, layer):
    t = proj.shape[0]
    nq = SEQ // ATT_TQ
    qw = A_GROUP * HEAD_DIM
    return pl.pallas_call(
        _attn_a_kernel,
        grid=(bsz, A_KV_HEADS, nq),
        in_specs=[
            pl.BlockSpec((ATT_TQ, qw), lambda b, g, i: (b * nq + i, COL_AQ // qw + g)),
            pl.BlockSpec((SEQ, HEAD_DIM), lambda b, g, i: (b, COL_AK // HEAD_DIM + g)),
            pl.BlockSpec((SEQ, HEAD_DIM), lambda b, g, i: (b, COL_AV // HEAD_DIM + g)),
            pl.BlockSpec((ATT_TQ, HEAD_DIM), lambda b, g, i: (i, 0)),
            pl.BlockSpec((ATT_TQ, HEAD_DIM), lambda b, g, i: (i, 0)),
            pl.BlockSpec((SEQ, HEAD_DIM), lambda b, g, i: (0, 0)),
            pl.BlockSpec((SEQ, HEAD_DIM), lambda b, g, i: (0, 0)),
            pl.BlockSpec((None, 2, HEAD_DIM), lambda b, g, i: (layer, 0, 0)),
        ],
        out_specs=pl.BlockSpec((ATT_TQ, qw), lambda b, g, i: (b * nq + i, g)),
        out_shape=jax.ShapeDtypeStruct((t, A_Q), BF16),
        scratch_shapes=[pltpu.VMEM((SEQ, HEAD_DIM), BF16), pltpu.VMEM((SEQ, 2 * HEAD_DIM), BF16)],
        compiler_params=_params("parallel", "arbitrary", "arbitrary"),
        name="attn_a",
    )(proj, proj, proj, cos, sin_signed, cos, sin_signed, qk_gain)


def _nb_key_row_start(r):
    rows = SEQ // GRID_W
    return int(np.clip(r * NB_QROWS - NA_ROWS // 2, 0, rows - NB_KROWS))


def _attn_b_kernel(q_ref, k_ref, v_ref, bias_ref, o_ref, v1_ref):
    scale = HEAD_DIM ** -0.5 * LOG2E
    nr = SEQ // NB_TQ
    key_start = [_nb_key_row_start(r) * GRID_W for r in range(nr)]
    _with_ones(v_ref, v1_ref)

    def scores(r):
        q = (q_ref[r * NB_TQ:(r + 1) * NB_TQ, :].astype(F32) * scale).astype(BF16)
        return _dot_nt(q, k_ref[key_start[r]:key_start[r] + NB_TK, :]) + bias_ref[0, r]

    s_next = scores(0)
    for r in range(nr):
        s = s_next
        if r + 1 < nr:
            s_next = scores(r + 1)
        o = _softmax_pv(s, v1_ref[key_start[r]:key_start[r] + NB_TK, :])
        o_ref[r * NB_TQ:(r + 1) * NB_TQ, :] = o.astype(BF16)


def _attn_b(proj, bsz, bias, layer):
    t = proj.shape[0]
    nr = SEQ // NB_TQ
    return pl.pallas_call(
        _attn_b_kernel,
        grid=(B_HEADS, bsz),
        in_specs=[
            pl.BlockSpec((SEQ, HEAD_DIM), lambda h, b: (b, COL_BQ // HEAD_DIM + h)),
            pl.BlockSpec((SEQ, HEAD_DIM), lambda h, b: (b, COL_BK // HEAD_DIM + h)),
            pl.BlockSpec((SEQ, HEAD_DIM), lambda h, b: (b, COL_BV // HEAD_DIM + h)),
            pl.BlockSpec((None, 1, nr, NB_TQ, NB_TK), lambda h, b: (layer, h, 0, 0, 0)),
        ],
        out_specs=pl.BlockSpec((SEQ, HEAD_DIM), lambda h, b: (b, h)),
        out_shape=jax.ShapeDtypeStruct((t, B_W), BF16),
        scratch_shapes=[pltpu.VMEM((SEQ, 2 * HEAD_DIM), BF16)],
        compiler_params=_params("arbitrary", "arbitrary"),
        name="attn_b",
    )(proj, proj, proj, bias)


def _nb_bias_tables(rpb):
    rows = SEQ // GRID_W
    nr = SEQ // NB_TQ
    n_dr, n_dc = 2 * NA_ROWS - 1, 2 * NA_COLS - 1
    row_sel = np.zeros((nr, NB_QROWS, NB_KROWS, n_dr), np.float32)
    for r in range(nr):
        for rq in range(NB_QROWS):
            gq = r * NB_QROWS + rq
            rs = int(np.clip(gq - NA_ROWS // 2, 0, rows - NA_ROWS))
            for rk in range(NB_KROWS):
                gk = _nb_key_row_start(r) + rk
                if rs <= gk < rs + NA_ROWS:
                    row_sel[r, rq, rk, gk - gq + NA_ROWS - 1] = 1.0
    col_sel = np.zeros((GRID_W, GRID_W, n_dc), np.float32)
    for cq in range(GRID_W):
        cs = int(np.clip(cq - NA_COLS // 2, 0, GRID_W - NA_COLS))
        for ck in range(cs, cs + NA_COLS):
            col_sel[cq, ck, ck - cq + NA_COLS - 1] = 1.0
    valid = np.einsum("rqkd,xye->rqxky", row_sel, col_sel) > 0.5
    t1 = jnp.einsum("rqkd,lhde->lhrqke", jnp.asarray(row_sel), rpb, precision=lax.Precision.HIGHEST)
    tab = jnp.einsum("lhrqke,xye->lhrqxky", t1, jnp.asarray(col_sel), precision=lax.Precision.HIGHEST)
    tab = jnp.where(jnp.asarray(valid)[None, None], tab * LOG2E, NEG)
    return tab.reshape(rpb.shape[0], B_HEADS, nr, NB_TQ, NB_TK)


def _gla_scan_matrices():
    c, nl = GLA_C, GLA_LEVELS
    m = np.zeros((2, nl + 2, c, c), np.float32)
    for d in range(2):
        tau = np.arange(c) if d == 0 else c - 1 - np.arange(c)
        tx, ty = tau[:, None], tau[None, :]
        for l in range(nl):
            left_end = ((tx >> (l + 1)) << (l + 1)) + (1 << l) - 1
            right = ((tx >> l) & 1) == 1
            m[d, l] = np.where(right, (ty > left_end) & (ty <= tx), (ty > tx) & (ty <= left_end))
        m[d, nl] = ty <= tx
        m[d, nl + 1] = ty > tx
    return jnp.asarray(m.reshape(2, (nl + 2) * c, c), BF16)


def _gla_kernel(q_ref, k_ref, v_ref, og_ref, lr_ref, wd_ref, bd_ref, onorm_ref, m_ref, o_ref,
                acc_ref, e_ref, st_ref):
    c, nl = GLA_C, GLA_LEVELS
    nch = SEQ // c
    lanes = 2 * C_DK

    row = lax.broadcasted_iota(jnp.int32, (c, lanes), 0)
    head_of_lane = lax.broadcasted_iota(jnp.int32, (c, lanes), 1) >> 6
    xr = ((lax.broadcasted_iota(jnp.int32, (2 * c, c), 0) & (c - 1))
          ^ lax.broadcasted_iota(jnp.int32, (2 * c, c), 1))

    st_ref[...] = jnp.zeros_like(st_ref)

    def stack_heads(per_head):
        return jnp.concatenate([x.astype(BF16) for x in per_head], axis=0)

    def scan_chunk(pair, d, ci):
        c_phys = ci if d == 0 else nch - 1 - ci
        rows = pl.ds(pl.multiple_of(c_phys * c, c), c)
        pair_lanes = slice(pair * lanes, (pair + 1) * lanes)
        qf = q_ref[rows, pair_lanes].astype(F32) * (C_DK ** -0.5)
        kf = k_ref[rows, pair_lanes].astype(F32)
        z = (jnp.dot(lr_ref[rows, :], wd_ref[d, :, pair_lanes], preferred_element_type=F32)
             + bd_ref[d, :, pair_lanes])
        g = (jnp.minimum(z, 0.0) - jnp.log(1.0 + jnp.exp(-jnp.abs(z)))) * (1.0 / C_TAU)
        g_hi = g.astype(BF16)
        g_lo = (g - g_hi.astype(F32)).astype(BF16)
        e2 = jnp.dot(m_ref[d], jnp.concatenate([g_hi, g_lo], axis=1), preferred_element_type=F32)
        e_ref[pair, d] = e2[:, :lanes] + e2[:, lanes:]
        yield

        q_head = [jnp.where(head_of_lane == h, qf, 0.0) for h in range(2)]
        a = jnp.where(xr == 0, _dot_nt(stack_heads(q_head), kf.astype(BF16)), 0.0)
        yield
        for l in range(nl):
            t = jnp.exp(e_ref[pair, d, l * c:(l + 1) * c, :])
            is_query_row = ((row >> l) & 1) == (1 - d)
            k_l = jnp.where(is_query_row, 0.0, kf * t).astype(BF16)
            q_l = stack_heads([jnp.where(is_query_row, q * t, 0.0) for q in q_head])
            p = _dot_nt(q_l, k_l)
            a = a + (jnp.where(xr < (2 << l), p, 0.0) if l < nl - 1 else p)
            yield

        t_in = jnp.exp(e_ref[pair, d, nl * c:(nl + 1) * c, :])
        t_out = jnp.exp(e_ref[pair, d, (nl + 1) * c:(nl + 2) * c, :])
        k_out = (kf * t_out).astype(BF16)
        q_in = stack_heads([q * t_in for q in q_head])
        a_bf = a.astype(BF16)
        last = c - 1 if d == 0 else 0
        decay_all = t_in[last:last + 1, :]
        yield
        for h in range(2):
            head = 2 * pair + h
            cols = slice(head * HEAD_DIM, (head + 1) * HEAD_DIM)
            v_h = v_ref[rows, cols]
            st = st_ref[d, head]
            acc_ref[d, rows, cols] = (
                jnp.dot(a_bf[h * c:(h + 1) * c], v_h, preferred_element_type=F32)
                + _dot_nt(q_in[h * c:(h + 1) * c], st.astype(BF16)))
            st_ref[d, head] = st * decay_all + _dot_tn(v_h, k_out)
            yield

    def scan_step(ci, carry):
        scans = [scan_chunk(pair, d, ci) for pair in range(2) for d in range(2)]
        while scans:
            scans = [s for s in scans if next(s, "done") is None]
        return carry

    lax.fori_loop(0, nch, scan_step, 0)

    def finish(ci, carry):
        rows = pl.ds(pl.multiple_of(ci * c, c), c)
        for h in range(C_V // HEAD_DIM):
            cols = slice(h * HEAD_DIM, (h + 1) * HEAD_DIM)
            y = _rms(acc_ref[0, rows, cols] + acc_ref[1, rows, cols], onorm_ref[...])
            gate = og_ref[rows, cols].astype(F32)
            o_ref[rows, cols] = (y * gate * _sigmoid(gate)).astype(BF16)
        return carry

    lax.fori_loop(0, nch, finish, 0)


def _gla(proj, bsz, w_dec, b_dec, onorm, scan_m, layer):
    t = proj.shape[0]
    lanes = 2 * C_DK
    n_e = (GLA_LEVELS + 2) * GLA_C
    return pl.pallas_call(
        _gla_kernel,
        grid=(bsz,),
        in_specs=[
            pl.BlockSpec((SEQ, C_K), lambda b: (b, COL_CQ // C_K)),
            pl.BlockSpec((SEQ, C_K), lambda b: (b, COL_CK // C_K)),
            pl.BlockSpec((SEQ, C_V), lambda b: (b, COL_CV // C_V)),
            pl.BlockSpec((SEQ, C_V), lambda b: (b, COL_COG // C_V)),
            pl.BlockSpec((SEQ, lanes), lambda b: (b, COL_CL // lanes)),
            pl.BlockSpec((None, 2, lanes, C_K), lambda b: (layer, 0, 0, 0)),
            pl.BlockSpec((None, 2, 1, C_K), lambda b: (layer, 0, 0, 0)),
            pl.BlockSpec((None, 1, HEAD_DIM), lambda b: (layer, 0, 0)),
            pl.BlockSpec((2, n_e, GLA_C), lambda b: (0, 0, 0)),
        ],
        out_specs=pl.BlockSpec((SEQ, C_V), lambda b: (b, 0)),
        out_shape=jax.ShapeDtypeStruct((t, C_V), BF16),
        scratch_shapes=[
            pltpu.VMEM((2, SEQ, C_V), F32),
            pltpu.VMEM((2, 2, n_e, lanes), F32),
            pltpu.VMEM((2, C_V // HEAD_DIM, HEAD_DIM, lanes), F32),
        ],
        compiler_params=_params("parallel"),
        name="gla",
    )(proj, proj, proj, proj, proj, w_dec, b_dec, onorm, scan_m)


def _merge_kernel(x_ref, oa_ref, ob_ref, oc_ref, ga_ref, gb_ref, gc_ref, gbias_ref,
                  wbr_ref, wo_ref, ng_ref, o_ref, m_ref):
    j = pl.program_id(1)
    tn = MERGE_TN
    n_early = D_MODEL // tn - 1
    branch_rows = ((oa_ref, 0, A_Q), (ob_ref, A_Q, A_Q + B_W), (oc_ref, A_Q + B_W, D_MODEL))

    def branch_products(tile):
        return [jnp.dot(o_b[...], wbr_ref[tile, lo:hi, :], preferred_element_type=F32)
                for o_b, lo, hi in branch_rows]

    def gated_sum(ys, tile):
        gates = (ga_ref, gb_ref, gc_ref)
        bias = gbias_ref[tile]
        return sum(_sigmoid(gates[b][...].astype(F32) + bias[b:b + 1, :]) * ys[b]
                   for b in range(3)).astype(BF16)

    @pl.when(j < n_early)
    def _():
        m_ref[j] = gated_sum(branch_products(j), j)

    @pl.when(j == n_early)
    def _():
        ys = branch_products(n_early)
        acc = jnp.dot(m_ref[0], wo_ref[0:tn, :], preferred_element_type=F32)
        for t in range(1, n_early):
            acc += jnp.dot(m_ref[t], wo_ref[t * tn:(t + 1) * tn, :], preferred_element_type=F32)
        acc += jnp.dot(gated_sum(ys, n_early), wo_ref[n_early * tn:, :], preferred_element_type=F32)
        o_ref[...] = x_ref[...] + _rms(acc, ng_ref[3:4, :])


def _merge(x, oa, ob, oc, proj, gate_bias, w_br, wo, ng, layer):
    t = x.shape[0]
    tm, tn = MERGE_TM, MERGE_TN
    nj = D_MODEL // tn
    return pl.pallas_call(
        _merge_kernel,
        grid=(t // tm, D_MODEL // tn),
        in_specs=[
            pl.BlockSpec((tm, D_MODEL), lambda i, j: (i, 0)),
            pl.BlockSpec((tm, A_Q), lambda i, j: (i, 0)),
            pl.BlockSpec((tm, B_W), lambda i, j: (i, 0)),
            pl.BlockSpec((tm, C_V), lambda i, j: (i, 0)),
            pl.BlockSpec((tm, tn), lambda i, j: (i, COL_GA // tn + j)),
            pl.BlockSpec((tm, tn), lambda i, j: (i, COL_GB // tn + j)),
            pl.BlockSpec((tm, tn), lambda i, j: (i, COL_GC // tn + j)),
            pl.BlockSpec((None, nj, 3, tn), lambda i, j: (layer, 0, 0, 0)),
            pl.BlockSpec((None, nj, D_MODEL, tn), lambda i, j: (layer, 0, 0, 0)),
            pl.BlockSpec((None, D_MODEL, D_MODEL), lambda i, j: (layer, 0, 0)),
            pl.BlockSpec((None, 6, D_MODEL), lambda i, j: (layer, 0, 0)),
        ],
        out_specs=pl.BlockSpec((tm, D_MODEL), lambda i, j: (i, 0)),
        out_shape=jax.ShapeDtypeStruct((t, D_MODEL), F32),
        scratch_shapes=[pltpu.VMEM((D_MODEL // tn - 1, tm, tn), BF16)],
        compiler_params=_params("parallel", "arbitrary"),
        name="merge",
    )(x, oa, ob, oc, proj, proj, proj, gate_bias, w_br, wo, ng)


def _rope_tables():
    t = jnp.arange(SEQ)
    half = HEAD_DIM // 2
    inv = ROPE_THETA ** (-jnp.arange(0, half, 2, dtype=F32) / half)
    pos_r = (t // GRID_W).astype(F32)
    pos_c = (t % GRID_W).astype(F32)
    ang = jnp.concatenate([pos_r[:, None] * inv, pos_c[:, None] * inv], axis=-1)
    cos = jnp.repeat(jnp.cos(ang), 2, axis=-1)
    sign = jnp.asarray(np.tile(np.array([-1.0, 1.0], np.float32), half))
    sin_signed = jnp.repeat(jnp.sin(ang), 2, axis=-1) * sign
    return cos.astype(F32), sin_signed.astype(F32)


def _relayout_w_in(w_in):
    n_head = COL_CL + 2 * C_RANK
    w = w_in.astype(BF16)
    pad = jnp.zeros(w.shape[:2] + (COL_GA - n_head,), BF16)
    return jnp.concatenate([w[..., :n_head], pad, w[..., n_head:]], axis=-1)


def _merge_tiles(w_br_a, w_br_b, w_br_c, gate_bias):
    nj = D_MODEL // MERGE_TN
    w = jnp.concatenate([w_br_a, w_br_b, w_br_c], axis=1).astype(BF16)
    n_layers = w.shape[0]
    w = w.reshape(n_layers, D_MODEL, nj, MERGE_TN).transpose(0, 2, 1, 3)
    bias = gate_bias.reshape(n_layers, 3, nj, MERGE_TN).transpose(0, 2, 1, 3)
    return w, bias


def _decay_weights(w_decay, b_decay):
    lanes = 2 * C_DK
    zeros = jnp.zeros((w_decay.shape[0], lanes - 2 * C_RANK, C_K), w_decay.dtype)
    zero_r = jnp.zeros((w_decay.shape[0], C_RANK, C_K), w_decay.dtype)
    w_f = jnp.concatenate([w_decay[:, 0], zero_r, zeros], axis=1)
    w_b = jnp.concatenate([zero_r, w_decay[:, 1], zeros], axis=1)
    return jnp.stack([w_f, w_b], axis=1).astype(BF16), b_decay[:, :, None, :].astype(F32)


def kernel(x_prompt, x_sample, norm_gains, w_in, gate_bias, qk_norm_a, rpb_b, w_decay_c, b_decay_c,
           onorm_c, w_br_a, w_br_b, w_br_c, w_out, w_ffn1_in, w_ffn1_out, w_ffn2_in, w_ffn2_out):
    assert x_prompt.shape[1:] == (SEQ, D_MODEL) and x_sample.shape[1:] == (SEQ, D_MODEL)
    cos, sin_signed = _rope_tables()
    w_proj = _relayout_w_in(w_in)
    w_dec, b_dec = _decay_weights(w_decay_c, b_decay_c)
    nb_bias = _nb_bias_tables(rpb_b)
    scan_m = _gla_scan_matrices()
    w_br, gate_bias_t = _merge_tiles(w_br_a, w_br_b, w_br_c, gate_bias)
    wo = w_out.astype(BF16)
    f1i, f1o, f2i, f2o = (w.astype(BF16) for w in (w_ffn1_in, w_ffn1_out, w_ffn2_in, w_ffn2_out))
    onorm = onorm_c[:, None, :]

    def trunk(x):
        bsz = x.shape[0]
        x = x.reshape(bsz * SEQ, D_MODEL)
        for l in range(DEPTH):
            x = _ffn(x, norm_gains, f1i, f1o, l, 0, 1)
            proj = _proj(x, norm_gains, w_proj, l)
            oa = _attn_a(proj, bsz, cos, sin_signed, qk_norm_a, l)
            ob = _attn_b(proj, bsz, nb_bias, l)
            oc = _gla(proj, bsz, w_dec, b_dec, onorm, scan_m, l)
            x = _merge(x, oa, ob, oc, proj, gate_bias_t, w_br, wo, norm_gains, l)
            x = _ffn(x, norm_gains, f2i, f2o, l, 4, 5)
        return x.reshape(bsz, SEQ, D_MODEL)

    return trunk(x_prompt), trunk(x_sample)
```

```python
import functools
import math

import jax
import jax.numpy as jnp
import numpy as np
from jax import lax
from jax.experimental import pallas as pl
from jax.experimental.pallas import tpu as pltpu

F32 = jnp.float32
BF16 = jnp.bfloat16

D_MODEL = 2048
SEQ = 2048
DEPTH = 4
GRID_W = 64
HEAD_DIM = 128
A_Q = 1024
A_KV = 256
A_GROUP = 4
A_KV_HEADS = 2
ROPE_THETA = 10000.0
B_W = 512
B_HEADS = 4
NA_ROWS = 8
NA_COLS = 16
C_V = 512
C_K = 256
C_DK = 64
C_RANK = 16
C_TAU = 16.0
D_FF = 5632
EPS = 1e-6

COL_AQ, COL_AK, COL_AV = 0, 1024, 1280
COL_BQ, COL_BK, COL_BV = 1536, 2048, 2560
COL_CQ, COL_CK, COL_CV, COL_COG = 3072, 3328, 3584, 4096
COL_CL = 4608
COL_GA, COL_GB, COL_GC = 5120, 7168, 9216
N_PROJ = 11264

NEG = -1e30
LOG2E = math.log2(math.e)
VMEM_LIMIT = 56 * 1024 * 1024

FFN_TM, FFN_TF = 1024, 256
PROJ_TM, PROJ_TN = 1024, 1024
MERGE_TM, MERGE_TN = 512, 512
FFN_VMEM_LIMIT = 60 * 1024 * 1024
ATT_TQ = 256
NB_QROWS = 8
NB_KROWS = 16
NB_TQ = NB_QROWS * GRID_W
NB_TK = NB_KROWS * GRID_W
GLA_C = 128
GLA_LEVELS = GLA_C.bit_length() - 1


def _rms(x, gain):
    ms = jnp.mean(x * x, axis=-1, keepdims=True)
    return x * lax.rsqrt(ms + EPS) * gain


def _sigmoid(x):
    return 1.0 / (1.0 + jnp.exp(-x))


def _dot_nt(a, b):
    return lax.dot_general(a, b, (((1,), (1,)), ((), ())), preferred_element_type=F32)


def _dot_tn(a, b):
    return lax.dot_general(a, b, (((0,), (0,)), ((), ())), preferred_element_type=F32)


def _params(*sem, vmem_limit=VMEM_LIMIT):
    return pltpu.CompilerParams(dimension_semantics=sem, vmem_limit_bytes=vmem_limit)


def _ffn_kernel(x_ref, ng_ref, wg_ref, wu_ref, wo_ref, o_ref, xn_ref, *, row_in, row_out):
    f = pl.program_id(1)

    @pl.when(f == 0)
    def _():
        xn_ref[...] = _rms(x_ref[...], ng_ref[row_in:row_in + 1, :]).astype(BF16)
        o_ref[...] = jnp.zeros_like(o_ref)

    xn = xn_ref[...]
    g = jnp.dot(xn, wg_ref[...], preferred_element_type=F32)
    u = jnp.dot(xn, wu_ref[...], preferred_element_type=F32)
    a = (g * _sigmoid(g) * u).astype(BF16)
    o_ref[...] += jnp.dot(a, wo_ref[...], preferred_element_type=F32)

    @pl.when(f == pl.num_programs(1) - 1)
    def _():
        y = _rms(o_ref[...], ng_ref[row_out:row_out + 1, :])
        o_ref[...] = x_ref[...] + 0.5 * y


def _ffn(x, ng, w_in, w_out, layer, row_in, row_out):
    t = x.shape[0]
    nf = D_FF // FFN_TF
    return pl.pallas_call(
        functools.partial(_ffn_kernel, row_in=row_in, row_out=row_out),
        grid=(t // FFN_TM, nf),
        in_specs=[
            pl.BlockSpec((FFN_TM, D_MODEL), lambda i, f: (i, 0)),
            pl.BlockSpec((None, 6, D_MODEL), lambda i, f: (layer, 0, 0)),
            pl.BlockSpec((None, D_MODEL, FFN_TF), lambda i, f: (layer, 0, f)),
            pl.BlockSpec((None, D_MODEL, FFN_TF), lambda i, f: (layer, 0, f + nf)),
            pl.BlockSpec((None, FFN_TF, D_MODEL), lambda i, f: (layer, f, 0)),
        ],
        out_specs=pl.BlockSpec((FFN_TM, D_MODEL), lambda i, f: (i, 0)),
        out_shape=jax.ShapeDtypeStruct((t, D_MODEL), F32),
        scratch_shapes=[pltpu.VMEM((FFN_TM, D_MODEL), BF16)],
        compiler_params=_params("parallel", "arbitrary", vmem_limit=FFN_VMEM_LIMIT),
        name="ffn",
    )(x, ng, w_in, w_in, w_out)


def _proj_kernel(x_ref, ng_ref, w_ref, o_ref, xn_ref):
    @pl.when(pl.program_id(1) == 0)
    def _():
        xn_ref[...] = _rms(x_ref[...], ng_ref[2:3, :]).astype(BF16)

    o_ref[...] = jnp.dot(xn_ref[...], w_ref[...], preferred_element_type=F32).astype(BF16)


def _proj(x, ng, w, layer):
    t = x.shape[0]
    return pl.pallas_call(
        _proj_kernel,
        grid=(t // PROJ_TM, N_PROJ // PROJ_TN),
        in_specs=[
            pl.BlockSpec((PROJ_TM, D_MODEL), lambda i, j: (i, 0)),
            pl.BlockSpec((None, 6, D_MODEL), lambda i, j: (layer, 0, 0)),
            pl.BlockSpec((None, D_MODEL, PROJ_TN), lambda i, j: (layer, 0, j)),
        ],
        out_specs=pl.BlockSpec((PROJ_TM, PROJ_TN), lambda i, j: (i, j)),
        out_shape=jax.ShapeDtypeStruct((t, N_PROJ), BF16),
        scratch_shapes=[pltpu.VMEM((PROJ_TM, D_MODEL), BF16)],
        compiler_params=_params("parallel", "arbitrary"),
        name="proj",
    )(x, ng, w)


def _norm_rope(x, gain, cos, sin_signed, even_lane):
    y = _rms(x, gain)
    swapped = jnp.where(even_lane, pltpu.roll(y, HEAD_DIM - 1, 1), pltpu.roll(y, 1, 1))
    return y * cos + swapped * sin_signed


def _with_ones(v_ref, v1_ref):
    v1_ref[:, :HEAD_DIM] = v_ref[...]
    v1_ref[:, HEAD_DIM:] = jnp.ones((v_ref.shape[0], HEAD_DIM), BF16)


def _softmax_pv(s, v1):
    m = jnp.max(s, axis=-1, keepdims=True)
    p = jnp.exp2(s - m).astype(BF16)
    ov = jnp.dot(p, v1, preferred_element_type=F32)
    return ov[:, :HEAD_DIM] / ov[:, HEAD_DIM:]


def _attn_a_kernel(q_ref, k_ref, v_ref, cq_ref, sq_ref, ck_ref, sk_ref, gain_ref, o_ref,
                   kn_ref, v1_ref):
    @pl.when(pl.program_id(2) == 0)
    def _():
        even_k = (lax.broadcasted_iota(jnp.int32, (SEQ, HEAD_DIM), 1) & 1) == 0
        k = k_ref[...].astype(F32)
        kn_ref[...] = _norm_rope(k, gain_ref[1:2, :], ck_ref[...], sk_ref[...], even_k).astype(BF16)
        _with_ones(v_ref, v1_ref)

    even_q = (lax.broadcasted_iota(jnp.int32, (ATT_TQ, HEAD_DIM), 1) & 1) == 0
    kn = kn_ref[...]
    v = v1_ref[...]
    cq = cq_ref[...]
    sq = sq_ref[...]
    scale = HEAD_DIM ** -0.5 * LOG2E

    def scores(h):
        q = q_ref[:, h * HEAD_DIM:(h + 1) * HEAD_DIM].astype(F32)
        qn = (_norm_rope(q, gain_ref[0:1, :], cq, sq, even_q) * scale).astype(BF16)
        return _dot_nt(qn, kn)

    s_next = scores(0)
    for h in range(A_GROUP):
        s = s_next
        if h + 1 < A_GROUP:
            s_next = scores(h + 1)
        o_ref[:, h * HEAD_DIM:(h + 1) * HEAD_DIM] = _softmax_pv(s, v).astype(BF16)


def _attn_a(proj, bsz, cos, sin_signed, qk_gain, layer):
    t = proj.shape[0]
    nq = SEQ // ATT_TQ
    qw = A_GROUP * HEAD_DIM
    return pl.pallas_call(
        _attn_a_kernel,
        grid=(bsz, A_KV_HEADS, nq),
        in_specs=[
            pl.BlockSpec((ATT_TQ, qw), lambda b, g, i: (b * nq + i, COL_AQ // qw + g)),
            pl.BlockSpec((SEQ, HEAD_DIM), lambda b, g, i: (b, COL_AK // HEAD_DIM + g)),
            pl.BlockSpec((SEQ, HEAD_DIM), lambda b, g, i: (b, COL_AV // HEAD_DIM + g)),
            pl.BlockSpec((ATT_TQ, HEAD_DIM), lambda b, g, i: (i, 0)),
            pl.BlockSpec((ATT_TQ, HEAD_DIM), lambda b, g, i: (i, 0)),
            pl.BlockSpec((SEQ, HEAD_DIM), lambda b, g, i: (0, 0)),
            pl.BlockSpec((SEQ, HEAD_DIM), lambda b, g, i: (0, 0)),
            pl.BlockSpec((None, 2, HEAD_DIM), lambda b, g, i: (layer, 0, 0)),
        ],
        out_specs=pl.BlockSpec((ATT_TQ, qw), lambda b, g, i: (b * nq + i, g)),
        out_shape=jax.ShapeDtypeStruct((t, A_Q), BF16),
        scratch_shapes=[pltpu.VMEM((SEQ, HEAD_DIM), BF16), pltpu.VMEM((SEQ, 2 * HEAD_DIM), BF16)],
        compiler_params=_params("parallel", "arbitrary", "arbitrary"),
        name="attn_a",
    )(proj, proj, proj, cos, sin_signed, cos, sin_signed, qk_gain)


def _nb_key_row_start(r):
    rows = SEQ // GRID_W
    return int(np.clip(r * NB_QROWS - NA_ROWS // 2, 0, rows - NB_KROWS))


def _attn_b_kernel(q_ref, k_ref, v_ref, bias_ref, o_ref, v1_ref):
    scale = HEAD_DIM ** -0.5 * LOG2E
    nr = SEQ // NB_TQ
    key_start = [_nb_key_row_start(r) * GRID_W for r in range(nr)]
    _with_ones(v_ref, v1_ref)

    def scores(r):
        q = (q_ref[r * NB_TQ:(r + 1) * NB_TQ, :].astype(F32) * scale).astype(BF16)
        return _dot_nt(q, k_ref[key_start[r]:key_start[r] + NB_TK, :]) + bias_ref[0, r]

    s_next = scores(0)
    for r in range(nr):
        s = s_next
        if r + 1 < nr:
            s_next = scores(r + 1)
        o = _softmax_pv(s, v1_ref[key_start[r]:key_start[r] + NB_TK, :])
        o_ref[r * NB_TQ:(r + 1) * NB_TQ, :] = o.astype(BF16)


def _attn_b(proj, bsz, bias, layer):
    t = proj.shape[0]
    nr = SEQ // NB_TQ
    return pl.pallas_call(
        _attn_b_kernel,
        grid=(B_HEADS, bsz),
        in_specs=[
            pl.BlockSpec((SEQ, HEAD_DIM), lambda h, b: (b, COL_BQ // HEAD_DIM + h)),
            pl.BlockSpec((SEQ, HEAD_DIM), lambda h, b: (b, COL_BK // HEAD_DIM + h)),
            pl.BlockSpec((SEQ, HEAD_DIM), lambda h, b: (b, COL_BV // HEAD_DIM + h)),
            pl.BlockSpec((None, 1, nr, NB_TQ, NB_TK), lambda h, b: (layer, h, 0, 0, 0)),
        ],
        out_specs=pl.BlockSpec((SEQ, HEAD_DIM), lambda h, b: (b, h)),
        out_shape=jax.ShapeDtypeStruct((t, B_W), BF16),
        scratch_shapes=[pltpu.VMEM((SEQ, 2 * HEAD_DIM), BF16)],
        compiler_params=_params("arbitrary", "arbitrary"),
        name="attn_b",
    )(proj, proj, proj, bias)


def _nb_bias_tables(rpb):
    rows = SEQ // GRID_W
    nr = SEQ // NB_TQ
    n_dr, n_dc = 2 * NA_ROWS - 1, 2 * NA_COLS - 1
    row_sel = np.zeros((nr, NB_QROWS, NB_KROWS, n_dr), np.float32)
    for r in range(nr):
        for rq in range(NB_QROWS):
            gq = r * NB_QROWS + rq
            rs = int(np.clip(gq - NA_ROWS // 2, 0, rows - NA_ROWS))
            for rk in range(NB_KROWS):
                gk = _nb_key_row_start(r) + rk
                if rs <= gk < rs + NA_ROWS:
                    row_sel[r, rq, rk, gk - gq + NA_ROWS - 1] = 1.0
    col_sel = np.zeros((GRID_W, GRID_W, n_dc), np.float32)
    for cq in range(GRID_W):
        cs = int(np.clip(cq - NA_COLS // 2, 0, GRID_W - NA_COLS))
        for ck in range(cs, cs + NA_COLS):
            col_sel[cq, ck, ck - cq + NA_COLS - 1] = 1.0
    valid = np.einsum("rqkd,xye->rqxky", row_sel, col_sel) > 0.5
    t1 = jnp.einsum("rqkd,lhde->lhrqke", jnp.asarray(row_sel), rpb, precision=lax.Precision.HIGHEST)
    tab = jnp.einsum("lhrqke,xye->lhrqxky", t1, jnp.asarray(col_sel), precision=lax.Precision.HIGHEST)
    tab = tab.reshape(rpb.shape[0], B_HEADS, nr, NB_TQ, NB_TK)
    return jnp.where(jnp.asarray(valid.reshape(nr, NB_TQ, NB_TK))[None, None], tab * LOG2E, NEG)


def _gla_scan_matrices():
    c, nl = GLA_C, GLA_LEVELS
    m = np.zeros((2, nl + 2, c, c), np.float32)
    for d in range(2):
        tau = np.arange(c) if d == 0 else c - 1 - np.arange(c)
        tx, ty = tau[:, None], tau[None, :]
        for l in range(nl):
            left_end = ((tx >> (l + 1)) << (l + 1)) + (1 << l) - 1
            right = ((tx >> l) & 1) == 1
            m[d, l] = np.where(right, (ty > left_end) & (ty <= tx), (ty > tx) & (ty <= left_end))
        m[d, nl] = ty <= tx
        m[d, nl + 1] = ty > tx
    return jnp.asarray(m.reshape(2, (nl + 2) * c, c), BF16)


def _gla_kernel(q_ref, k_ref, v_ref, og_ref, lr_ref, wd_ref, bd_ref, onorm_ref, m_ref, o_ref,
                acc_ref, e_ref, st_ref):
    c, nl = GLA_C, GLA_LEVELS
    nch = SEQ // c
    lanes = 2 * C_DK

    row = lax.broadcasted_iota(jnp.int32, (c, lanes), 0)
    head_of_lane = lax.broadcasted_iota(jnp.int32, (c, lanes), 1) >> 6
    xr = ((lax.broadcasted_iota(jnp.int32, (2 * c, c), 0) & (c - 1))
          ^ lax.broadcasted_iota(jnp.int32, (2 * c, c), 1))

    st_ref[...] = jnp.zeros_like(st_ref)

    def stack_heads(per_head):
        return jnp.concatenate([x.astype(BF16) for x in per_head], axis=0)

    def scan_chunk(pair, d, ci, other_done):
        c_phys = ci if d == 0 else nch - 1 - ci
        rows = pl.ds(pl.multiple_of(c_phys * c, c), c)
        pair_lanes = slice(pair * lanes, (pair + 1) * lanes)
        qf = q_ref[rows, pair_lanes].astype(F32) * (C_DK ** -0.5)
        kf = k_ref[rows, pair_lanes].astype(F32)
        z = (jnp.dot(lr_ref[rows, :], wd_ref[d, :, pair_lanes], preferred_element_type=F32)
             + bd_ref[d, :, pair_lanes])
        g = (jnp.minimum(z, 0.0) - jnp.log(1.0 + jnp.exp(-jnp.abs(z)))) * (1.0 / C_TAU)
        g_hi = g.astype(BF16)
        g_lo = (g - g_hi.astype(F32)).astype(BF16)
        e2 = jnp.dot(m_ref[d], jnp.concatenate([g_hi, g_lo], axis=1), preferred_element_type=F32)
        e_ref[pair, d] = e2[:, :lanes] + e2[:, lanes:]
        yield

        q_head = [jnp.where(head_of_lane == h, qf, 0.0) for h in range(2)]
        a = jnp.where(xr == 0, _dot_nt(stack_heads(q_head), kf.astype(BF16)), 0.0)
        yield
        for l in range(nl):
            t = jnp.exp(e_ref[pair, d, l * c:(l + 1) * c, :])
            is_query_row = ((row >> l) & 1) == (1 - d)
            k_l = jnp.where(is_query_row, 0.0, kf * t).astype(BF16)
            q_l = stack_heads([jnp.where(is_query_row, q * t, 0.0) for q in q_head])
            p = _dot_nt(q_l, k_l)
            a = a + (jnp.where(xr < (2 << l), p, 0.0) if l < nl - 1 else p)
            yield

        t_in = jnp.exp(e_ref[pair, d, nl * c:(nl + 1) * c, :])
        t_out = jnp.exp(e_ref[pair, d, (nl + 1) * c:(nl + 2) * c, :])
        k_out = (kf * t_out).astype(BF16)
        q_in = stack_heads([q * t_in for q in q_head])
        a_bf = a.astype(BF16)
        last = c - 1 if d == 0 else 0
        decay_all = t_in[last:last + 1, :]
        yield
        for h in range(2):
            head = 2 * pair + h
            cols = slice(head * HEAD_DIM, (head + 1) * HEAD_DIM)
            v_h = v_ref[rows, cols]
            st = st_ref[d, head]
            o_h = (jnp.dot(a_bf[h * c:(h + 1) * c], v_h, preferred_element_type=F32)
                   + _dot_nt(q_in[h * c:(h + 1) * c], st.astype(BF16)))
            st_ref[d, head] = st * decay_all + _dot_tn(v_h, k_out)
            if other_done:
                y = _rms(o_h + acc_ref[1 - d, rows, cols], onorm_ref[...])
                gate = og_ref[rows, cols].astype(F32)
                o_ref[rows, cols] = (y * gate * _sigmoid(gate)).astype(BF16)
            else:
                acc_ref[d, rows, cols] = o_h
            yield

    def scan_step(other_done, ci, carry):
        scans = [scan_chunk(pair, d, ci, other_done) for pair in range(2) for d in range(2)]
        while scans:
            scans = [s for s in scans if next(s, "done") is None]
        return carry

    lax.fori_loop(0, nch // 2, functools.partial(scan_step, False), 0)
    lax.fori_loop(nch // 2, nch, functools.partial(scan_step, True), 0)


def _gla(proj, bsz, w_dec, b_dec, onorm, scan_m, layer):
    t = proj.shape[0]
    lanes = 2 * C_DK
    n_e = (GLA_LEVELS + 2) * GLA_C
    return pl.pallas_call(
        _gla_kernel,
        grid=(bsz,),
        in_specs=[
            pl.BlockSpec((SEQ, C_K), lambda b: (b, COL_CQ // C_K)),
            pl.BlockSpec((SEQ, C_K), lambda b: (b, COL_CK // C_K)),
            pl.BlockSpec((SEQ, C_V), lambda b: (b, COL_CV // C_V)),
            pl.BlockSpec((SEQ, C_V), lambda b: (b, COL_COG // C_V)),
            pl.BlockSpec((SEQ, lanes), lambda b: (b, COL_CL // lanes)),
            pl.BlockSpec((None, 2, lanes, C_K), lambda b: (layer, 0, 0, 0)),
            pl.BlockSpec((None, 2, 1, C_K), lambda b: (layer, 0, 0, 0)),
            pl.BlockSpec((None, 1, HEAD_DIM), lambda b: (layer, 0, 0)),
            pl.BlockSpec((2, n_e, GLA_C), lambda b: (0, 0, 0)),
        ],
        out_specs=pl.BlockSpec((SEQ, C_V), lambda b: (b, 0)),
        out_shape=jax.ShapeDtypeStruct((t, C_V), BF16),
        scratch_shapes=[
            pltpu.VMEM((2, SEQ, C_V), F32),
            pltpu.VMEM((2, 2, n_e, lanes), F32),
            pltpu.VMEM((2, C_V // HEAD_DIM, HEAD_DIM, lanes), F32),
        ],
        compiler_params=_params("parallel"),
        name="gla",
    )(proj, proj, proj, proj, proj, w_dec, b_dec, onorm, scan_m)


def _merge_kernel(x_ref, oa_ref, ob_ref, oc_ref, ga_ref, gb_ref, gc_ref, gbias_ref,
                  wbr_ref, wo_ref, ng_ref, o_ref, m_ref):
    j = pl.program_id(1)
    tn = MERGE_TN
    n_early = D_MODEL // tn - 1
    branch_rows = ((oa_ref, 0, A_Q), (ob_ref, A_Q, A_Q + B_W), (oc_ref, A_Q + B_W, D_MODEL))

    half = tn // 2
    halves = (slice(0, half), slice(half, tn))

    def branch_products(tile):
        return [[jnp.dot(o_b[...], wbr_ref[tile, lo:hi, cols], preferred_element_type=F32)
                 for o_b, lo, hi in branch_rows] for cols in halves]

    def gated_sum(ys, tile):
        gates = (ga_ref, gb_ref, gc_ref)
        bias = gbias_ref[tile]
        parts = [sum(_sigmoid(gates[b][:, cols].astype(F32) + bias[b:b + 1, cols]) * ys_half[b]
                     for b in range(3)).astype(BF16)
                 for cols, ys_half in zip(halves, ys)]
        return jnp.concatenate(parts, axis=1)

    @pl.when(j < n_early)
    def _():
        m_ref[j] = gated_sum(branch_products(j), j)

    @pl.when(j == n_early)
    def _():
        ys = branch_products(n_early)
        acc = jnp.dot(m_ref[0], wo_ref[0:tn, :], preferred_element_type=F32)
        for t in range(1, n_early):
            acc += jnp.dot(m_ref[t], wo_ref[t * tn:(t + 1) * tn, :], preferred_element_type=F32)
        acc += jnp.dot(gated_sum(ys, n_early), wo_ref[n_early * tn:, :], preferred_element_type=F32)
        o_ref[...] = x_ref[...] + _rms(acc, ng_ref[3:4, :])


def _merge(x, oa, ob, oc, proj, gate_bias, w_br, wo, ng, layer):
    t = x.shape[0]
    tm, tn = MERGE_TM, MERGE_TN
    nj = D_MODEL // tn
    return pl.pallas_call(
        _merge_kernel,
        grid=(t // tm, D_MODEL // tn),
        in_specs=[
            pl.BlockSpec((tm, D_MODEL), lambda i, j: (i, 0)),
            pl.BlockSpec((tm, A_Q), lambda i, j: (i, 0)),
            pl.BlockSpec((tm, B_W), lambda i, j: (i, 0)),
            pl.BlockSpec((tm, C_V), lambda i, j: (i, 0)),
            pl.BlockSpec((tm, tn), lambda i, j: (i, COL_GA // tn + j)),
            pl.BlockSpec((tm, tn), lambda i, j: (i, COL_GB // tn + j)),
            pl.BlockSpec((tm, tn), lambda i, j: (i, COL_GC // tn + j)),
            pl.BlockSpec((None, nj, 3, tn), lambda i, j: (layer, 0, 0, 0)),
            pl.BlockSpec((None, nj, D_MODEL, tn), lambda i, j: (layer, 0, 0, 0)),
            pl.BlockSpec((None, D_MODEL, D_MODEL), lambda i, j: (layer, 0, 0)),
            pl.BlockSpec((None, 6, D_MODEL), lambda i, j: (layer, 0, 0)),
        ],
        out_specs=pl.BlockSpec((tm, D_MODEL), lambda i, j: (i, 0)),
        out_shape=jax.ShapeDtypeStruct((t, D_MODEL), F32),
        scratch_shapes=[pltpu.VMEM((D_MODEL // tn - 1, tm, tn), BF16)],
        compiler_params=_params("parallel", "arbitrary"),
        name="merge",
    )(x, oa, ob, oc, proj, proj, proj, gate_bias, w_br, wo, ng)


def _rope_tables():
    t = jnp.arange(SEQ)
    half = HEAD_DIM // 2
    inv = ROPE_THETA ** (-jnp.arange(0, half, 2, dtype=F32) / half)
    pos_r = (t // GRID_W).astype(F32)
    pos_c = (t % GRID_W).astype(F32)
    ang = jnp.concatenate([pos_r[:, None] * inv, pos_c[:, None] * inv], axis=-1)
    cos = jnp.repeat(jnp.cos(ang), 2, axis=-1)
    sign = jnp.asarray(np.tile(np.array([-1.0, 1.0], np.float32), half))
    sin_signed = jnp.repeat(jnp.sin(ang), 2, axis=-1) * sign
    return cos.astype(F32), sin_signed.astype(F32)


def _relayout_w_in(w_in):
    n_head = COL_CL + 2 * C_RANK
    w = w_in.astype(BF16)
    pad = jnp.zeros(w.shape[:2] + (COL_GA - n_head,), BF16)
    return jnp.concatenate([w[..., :n_head], pad, w[..., n_head:]], axis=-1)


def _merge_tiles(w_br_a, w_br_b, w_br_c, gate_bias):
    nj = D_MODEL // MERGE_TN
    w = jnp.concatenate([w_br_a, w_br_b, w_br_c], axis=1).astype(BF16)
    n_layers = w.shape[0]
    w = w.reshape(n_layers, D_MODEL, nj, MERGE_TN).transpose(0, 2, 1, 3)
    bias = gate_bias.reshape(n_layers, 3, nj, MERGE_TN).transpose(0, 2, 1, 3)
    return w, bias


def _decay_weights(w_decay, b_decay):
    lanes = 2 * C_DK
    zeros = jnp.zeros((w_decay.shape[0], lanes - 2 * C_RANK, C_K), w_decay.dtype)
    zero_r = jnp.zeros((w_decay.shape[0], C_RANK, C_K), w_decay.dtype)
    w_f = jnp.concatenate([w_decay[:, 0], zero_r, zeros], axis=1)
    w_b = jnp.concatenate([zero_r, w_decay[:, 1], zeros], axis=1)
    return jnp.stack([w_f, w_b], axis=1).astype(BF16), b_decay[:, :, None, :].astype(F32)


def kernel(x_prompt, x_sample, norm_gains, w_in, gate_bias, qk_norm_a, rpb_b, w_decay_c, b_decay_c,
           onorm_c, w_br_a, w_br_b, w_br_c, w_out, w_ffn1_in, w_ffn1_out, w_ffn2_in, w_ffn2_out):
    assert x_prompt.shape[1:] == (SEQ, D_MODEL) and x_sample.shape[1:] == (SEQ, D_MODEL)
    cos, sin_signed = _rope_tables()
    w_proj = _relayout_w_in(w_in)
    w_dec, b_dec = _decay_weights(w_decay_c, b_decay_c)
    nb_bias = _nb_bias_tables(rpb_b)
    scan_m = _gla_scan_matrices()
    w_br, gate_bias_t = _merge_tiles(w_br_a, w_br_b, w_br_c, gate_bias)
    wo = w_out.astype(BF16)
    f1i, f1o, f2i, f2o = (w.astype(BF16) for w in (w_ffn1_in, w_ffn1_out, w_ffn2_in, w_ffn2_out))
    onorm = onorm_c[:, None, :]

    def trunk(x):
        bsz = x.shape[0]
        x = x.reshape(bsz * SEQ, D_MODEL)
        for l in range(DEPTH):
            x = _ffn(x, norm_gains, f1i, f1o, l, 0, 1)
            proj = _proj(x, norm_gains, w_proj, l)
            oa = _attn_a(proj, bsz, cos, sin_signed, qk_norm_a, l)
            ob = _attn_b(proj, bsz, nb_bias, l)
            oc = _gla(proj, bsz, w_dec, b_dec, onorm, scan_m, l)
            x = _merge(x, oa, ob, oc, proj, gate_bias_t, w_br, wo, norm_gains, l)
            x = _ffn(x, norm_gains, f2i, f2o, l, 4, 5)
        return x.reshape(bsz, SEQ, D_MODEL)

    return trunk(x_prompt), trunk(x_sample)
```
